```python
import jax, jax.numpy as jnp
from jax import lax
import numpy as np

D_MODEL = 2048
BATCH = 1
SEQ = 16384
DEPTH = 1

PLE_DIM = 256
CONV_WIDTH = 2048
CONV_K = 3
LRU_WIDTH = 2048
LRU_HEADS = 8
LRU_HEAD_DIM = LRU_WIDTH // LRU_HEADS
LRU_CONV_K = 4
LRU_C = 8.0
N_EXPERTS = 32
TOP_K = 4
D_EXPERT = 2048
SWIGLU_LIMIT = 7.0
SWIGLU_ALPHA = 1.702
EXPERT_BLOCK = 256
NORM_EPS = 1e-6
IN_SPLITS = (CONV_WIDTH, CONV_WIDTH, CONV_WIDTH, LRU_WIDTH, LRU_WIDTH, D_MODEL, D_MODEL)
IN_WIDTH = 3 * CONV_WIDTH + 2 * LRU_WIDTH + 2 * D_MODEL

kernel_name = "hybrid_shortconv_rglru_moe_ple"


def rms_norm(x, g):
    xf = x.astype(jnp.float32)
    y = xf * lax.rsqrt(jnp.mean(xf * xf, axis=-1, keepdims=True) + NORM_EPS)
    return (y * g.astype(jnp.float32)).astype(x.dtype)


def causal_depthwise_conv(x, w):
    k, c = w.shape
    return lax.conv_general_dilated(
        x, w[:, None, :].astype(x.dtype), window_strides=(1,), padding=[(k - 1, 0)],
        dimension_numbers=("NWC", "WIO", "NWC"), feature_group_count=c)


def short_conv_mixer(b_gate, c_gate, v, conv_w, w_out):
    y = b_gate * causal_depthwise_conv(c_gate * v, conv_w)
    return y @ w_out


def rg_lru(x, w_rg, b_rg, w_ig, b_ig, lam):
    bsz, t, w = x.shape
    xf = x.astype(jnp.float32)
    xh = xf.reshape(bsz, t, LRU_HEADS, LRU_HEAD_DIM)
    r = jax.nn.sigmoid(jnp.einsum("bthi,hij->bthj", xh, w_rg.astype(jnp.float32)).reshape(bsz, t, w)
                       + b_rg.astype(jnp.float32))
    i = jax.nn.sigmoid(jnp.einsum("bthi,hij->bthj", xh, w_ig.astype(jnp.float32)).reshape(bsz, t, w)
                       + b_ig.astype(jnp.float32))
    log_a = -LRU_C * r * jax.nn.softplus(-lam.astype(jnp.float32))
    a = jnp.exp(log_a)
    mult = jnp.sqrt(-jnp.expm1(2.0 * log_a))
    mult = jnp.where((jnp.arange(t) == 0)[None, :, None], 1.0, mult)
    b_term = xf * i * mult

    def combine(left, right):
        a1, b1 = left
        a2, b2 = right
        return a1 * a2, a2 * b1 + b2

    _, h = lax.associative_scan(combine, (a, b_term), axis=1)
    return h.astype(x.dtype)


def recurrent_mixer(y_in, x_in, conv_w, conv_b, w_rg, b_rg, w_ig, b_ig, lam, w_out):
    y = jax.nn.gelu(y_in)
    xc = causal_depthwise_conv(x_in, conv_w) + conv_b
    h = rg_lru(xc, w_rg, b_rg, w_ig, b_ig, lam)
    return (h * y) @ w_out


def moe_ffn(x, w_router, b_router, w_gu, b_gu, w_down, b_down):
    bsz, t, d = x.shape
    n_tok = bsz * t
    xf = x.reshape(n_tok, d)
    logits = (xf @ w_router + b_router).astype(jnp.float32)
    top_val, top_idx = lax.top_k(logits, TOP_K)
    top_w = jax.nn.softmax(top_val, axis=-1)

    n_assign = n_tok * TOP_K
    n_blocks = -(-n_assign // EXPERT_BLOCK) + N_EXPERTS
    n_slots = n_blocks * EXPERT_BLOCK
    flat_e = top_idx.reshape(-1).astype(jnp.int32)
    flat_tok = jnp.repeat(jnp.arange(n_tok, dtype=jnp.int32), TOP_K)
    flat_w = top_w.reshape(-1)
    order = jnp.argsort(flat_e, stable=True)
    s_e, s_tok, s_w = flat_e[order], flat_tok[order], flat_w[order]
    counts = jnp.zeros((N_EXPERTS,), jnp.int32).at[flat_e].add(1)
    padded = (counts + EXPERT_BLOCK - 1) // EXPERT_BLOCK * EXPERT_BLOCK
    starts = jnp.cumsum(counts) - counts
    p_starts = jnp.cumsum(padded) - padded
    p_ends = p_starts + padded
    dest = p_starts[s_e] + jnp.arange(n_assign, dtype=jnp.int32) - starts[s_e]
    slot_tok = jnp.full((n_slots,), n_tok, jnp.int32).at[dest].set(s_tok)
    slot_w = jnp.zeros((n_slots,), jnp.float32).at[dest].set(s_w)
    block_e = jnp.minimum(
        jnp.searchsorted(p_ends, jnp.arange(n_blocks, dtype=jnp.int32) * EXPERT_BLOCK, side="right"),
        N_EXPERTS - 1).astype(jnp.int32)
    x_pad = jnp.concatenate([xf, jnp.zeros((1, d), xf.dtype)], axis=0)

    def expert_block(args):
        tok, e = args
        gu = x_pad[tok] @ w_gu[e] + b_gu[e]
        gate = jnp.minimum(gu[:, :D_EXPERT], SWIGLU_LIMIT)
        up = jnp.clip(gu[:, D_EXPERT:], -SWIGLU_LIMIT, SWIGLU_LIMIT)
        glu = gate * jax.nn.sigmoid(SWIGLU_ALPHA * gate)
        return ((up + 1.0) * glu) @ w_down[e] + b_down[e]

    out = lax.map(expert_block, (slot_tok.reshape(n_blocks, EXPERT_BLOCK), block_e))
    out = out.reshape(n_slots, d) * slot_w[:, None].astype(out.dtype)
    y = jnp.zeros((n_tok + 1, d), out.dtype).at[slot_tok].add(out)[:n_tok]
    return y.reshape(bsz, t, d)


def setup_inputs(seed: int = 0) -> dict:
    key = jax.random.key(seed)
    ks = jax.random.split(key, 32)
    f32 = jnp.float32
    nrm = lambda k, shape, fan_in: jax.random.normal(k, shape, f32) * (fan_in ** -0.5)
    small = lambda k, shape: 0.01 * jax.random.normal(k, shape, f32)
    gain = lambda k, shape: 1.0 + 0.02 * jax.random.normal(k, shape, f32)
    u = jax.random.uniform(ks[30], (DEPTH, LRU_WIDTH), f32, minval=0.9, maxval=0.999)
    a0 = u ** (1.0 / LRU_C)
    lru_lambda = jnp.log(a0) - jnp.log1p(-a0)
    return {
        "x": jax.random.normal(ks[0], (BATCH, SEQ, D_MODEL), f32),
        "p": jax.random.normal(ks[1], (DEPTH, BATCH, SEQ, PLE_DIM), f32),
        "norm_mix": gain(ks[2], (DEPTH, D_MODEL)),
        "w_in": nrm(ks[3], (DEPTH, D_MODEL, IN_WIDTH), D_MODEL),
        "conv_a_w": nrm(ks[4], (DEPTH, CONV_K, CONV_WIDTH), CONV_K),
        "w_a_out": nrm(ks[5], (DEPTH, CONV_WIDTH, D_MODEL), CONV_WIDTH),
        "conv_b_w": nrm(ks[6], (DEPTH, LRU_CONV_K, LRU_WIDTH), LRU_CONV_K),
        "conv_b_b": small(ks[7], (DEPTH, LRU_WIDTH)),
        "w_rg": nrm(ks[8], (DEPTH, LRU_HEADS, LRU_HEAD_DIM, LRU_HEAD_DIM), LRU_HEAD_DIM),
        "b_rg": small(ks[9], (DEPTH, LRU_WIDTH)),
        "w_ig": nrm(ks[10], (DEPTH, LRU_HEADS, LRU_HEAD_DIM, LRU_HEAD_DIM), LRU_HEAD_DIM),
        "b_ig": small(ks[11], (DEPTH, LRU_WIDTH)),
        "lru_lambda": lru_lambda,
        "w_b_out": nrm(ks[12], (DEPTH, LRU_WIDTH, D_MODEL), LRU_WIDTH),
        "w_o": nrm(ks[13], (DEPTH, D_MODEL, D_MODEL), D_MODEL),
        "norm_ffn": gain(ks[14], (DEPTH, D_MODEL)),
        "w_router": nrm(ks[15], (DEPTH, D_MODEL, N_EXPERTS), D_MODEL),
        "b_router": small(ks[16], (DEPTH, N_EXPERTS)),
        "w_gu": nrm(ks[17], (DEPTH, N_EXPERTS, D_MODEL, 2 * D_EXPERT), D_MODEL),
        "b_gu": small(ks[18], (DEPTH, N_EXPERTS, 2 * D_EXPERT)),
        "w_down": nrm(ks[19], (DEPTH, N_EXPERTS, D_EXPERT, D_MODEL), D_EXPERT),
        "b_down": small(ks[20], (DEPTH, N_EXPERTS, D_MODEL)),
        "norm_ple": gain(ks[21], (DEPTH, D_MODEL)),
        "w_ple_gate": nrm(ks[22], (DEPTH, D_MODEL, D_MODEL), D_MODEL),
        "w_ple_proj": nrm(ks[23], (DEPTH, PLE_DIM, D_MODEL), PLE_DIM),
        "norm_final": gain(ks[24], (D_MODEL,)),
    }


def reference(x, p, norm_mix, w_in, conv_a_w, w_a_out, conv_b_w, conv_b_b, w_rg, b_rg, w_ig, b_ig,
              lru_lambda, w_b_out, w_o, norm_ffn, w_router, b_router, w_gu, b_gu, w_down, b_down,
              norm_ple, w_ple_gate, w_ple_proj, norm_final):
    split_at = [int(s) for s in np.cumsum(IN_SPLITS)[:-1]]
    h = x
    for i in range(DEPTH):
        a = rms_norm(h, norm_mix[i])
        proj = a @ w_in[i]
        b_c, c_c, v_c, y_r, x_r, g_c, g_r = jnp.split(proj, split_at, axis=-1)
        y_conv = short_conv_mixer(b_c, c_c, v_c, conv_a_w[i], w_a_out[i])
        y_rec = recurrent_mixer(y_r, x_r, conv_b_w[i], conv_b_b[i], w_rg[i], b_rg[i], w_ig[i], b_ig[i],
                                lru_lambda[i], w_b_out[i])
        merged = jax.nn.sigmoid(g_c) * y_conv + jax.nn.sigmoid(g_r) * y_rec
        h = h + merged @ w_o[i]
        h = h + moe_ffn(rms_norm(h, norm_ffn[i]), w_router[i], b_router[i], w_gu[i], b_gu[i],
                        w_down[i], b_down[i])
        gate = jax.nn.sigmoid(rms_norm(h, norm_ple[i]) @ w_ple_gate[i])
        h = h + gate * (p[i] @ w_ple_proj[i])
    return rms_norm(h, norm_final)
```

```python
import functools

import jax
import jax.numpy as jnp
from jax import lax
from jax.experimental import pallas as pl
from jax.experimental.pallas import tpu as pltpu

F32 = jnp.float32
BF16 = jnp.bfloat16

NORM_EPS = 1e-6
LRU_HEADS = 8
LRU_C = 8.0
N_EXPERTS = 32
TOP_K = 4
SWIGLU_LIMIT = 7.0
SWIGLU_ALPHA = 1.702

LANES = 128
SUBLANES = 8
VMEM_LIMIT = 56 * 1024 * 1024

TM_MIX = 512
TM_OUT = 256
BR = 512
FC = 512
NEG_BIG = -3.0e38


def _rms(x, g):
    ms = jnp.mean(x * x, axis=-1, keepdims=True)
    return x * lax.rsqrt(ms + NORM_EPS) * g


def _dot(a, b):
    return jnp.dot(a, b, preferred_element_type=F32)


def _shift_rows(u, hist8, k):
    rolled = pltpu.roll(u, k, axis=0)
    hr = pltpu.roll(hist8, k, axis=0)
    row = lax.broadcasted_iota(jnp.int32, hist8.shape, 0)
    first = jnp.where(row < k, hr, rolled[:SUBLANES])
    return jnp.concatenate([first, rolled[SUBLANES:]], axis=0)


def _gelu_tanh(x):
    return 0.5 * x * (1.0 + jnp.tanh(0.7978845608028654 * (x + 0.044715 * (x * x * x))))


def _mix1_kernel(x_ref, gmix_ref, w5_ref, par_ref, wgate_ref,
                 a_ref, yc_ref, hy_ref,
                 hu_s, hx_s, hp_s, h_s):
    i = pl.program_id(0)
    c = pl.program_id(1)
    tm, cw = yc_ref.shape

    @pl.when(jnp.logical_and(i == 0, c == 0))
    def _():
        hu_s[...] = jnp.zeros_like(hu_s)
        hx_s[...] = jnp.zeros_like(hx_s)
        hp_s[...] = jnp.zeros_like(hp_s)

    @pl.when(c == 0)
    def _():
        a_ref[...] = _rms(x_ref[...], gmix_ref[...]).astype(BF16)

    proj = _dot(a_ref[...], w5_ref[0])
    b_c = proj[:, 0 * cw:1 * cw]
    c_c = proj[:, 1 * cw:2 * cw]
    v_c = proj[:, 2 * cw:3 * cw]
    y_r = proj[:, 3 * cw:4 * cw]
    x_r = proj[:, 4 * cw:5 * cw]
    par = par_ref[...]

    u = c_c * v_c
    hu = hu_s[c]
    conv = (par[0:1] * _shift_rows(u, hu, 2) + par[1:2] * _shift_rows(u, hu, 1) + par[2:3] * u)
    yc_ref[...] = (b_c * conv).astype(BF16)
    hu_s[c] = u[tm - SUBLANES:, :]

    hx = hx_s[c]
    xc = (par[3:4] * _shift_rows(x_r, hx, 3) + par[4:5] * _shift_rows(x_r, hx, 2)
          + par[5:6] * _shift_rows(x_r, hx, 1) + par[6:7] * x_r) + par[7:8]
    hx_s[c] = x_r[tm - SUBLANES:, :]
    gz = _dot(xc.astype(BF16), wgate_ref[0])
    r = jax.nn.sigmoid(gz[:, :cw] + par[8:9])
    ig = jax.nn.sigmoid(gz[:, cw:] + par[9:10])
    nlam = -par[10:11]
    softplus = jnp.maximum(nlam, 0.0) + jnp.log1p(jnp.exp(-jnp.abs(nlam)))
    log_a = (-LRU_C) * r * softplus
    a = jnp.exp(log_a)
    mult = jnp.sqrt(1.0 - a * a)
    trow = lax.broadcasted_iota(jnp.int32, (tm, cw), 0) + i * tm
    mult = jnp.where(trow == 0, 1.0, mult)
    b = xc * ig * mult

    sub = lax.broadcasted_iota(jnp.int32, (tm, cw), 0) % SUBLANES
    for s in (1, 2, 4):
        a_sh = pltpu.roll(a, s, axis=0)
        b_sh = pltpu.roll(b, s, axis=0)
        m = sub >= s
        b = jnp.where(m, a * b_sh + b, b)
        a = jnp.where(m, a * a_sh, a)
    carry = hp_s[c][SUBLANES - 1:SUBLANES, :]
    for g in range(tm // SUBLANES):
        lo = g * SUBLANES
        hg = b[lo:lo + SUBLANES, :] + a[lo:lo + SUBLANES, :] * carry
        h_s[lo:lo + SUBLANES, :] = hg
        carry = hg[SUBLANES - 1:SUBLANES, :]
    hp_s[c] = h_s[tm - SUBLANES:, :]
    hy_ref[...] = (h_s[...] * _gelu_tanh(y_r)).astype(BF16)


def _mix1(x2, gmix, w5, par, wgate):
    n, d = x2.shape
    nch, _, w5c = w5.shape
    cw = w5c // 5
    tm = TM_MIX
    grid = (n // tm, nch)
    return pl.pallas_call(
        _mix1_kernel,
        grid=grid,
        in_specs=[
            pl.BlockSpec((tm, d), lambda i, c: (i, 0)),
            pl.BlockSpec((1, d), lambda i, c: (0, 0)),
            pl.BlockSpec((1, d, w5c), lambda i, c: (c, 0, 0)),
            pl.BlockSpec((16, cw), lambda i, c: (0, c)),
            pl.BlockSpec((1, cw, 2 * cw), lambda i, c: (c, 0, 0)),
        ],
        out_specs=[
            pl.BlockSpec((tm, d), lambda i, c: (i, 0)),
            pl.BlockSpec((tm, cw), lambda i, c: (i, c)),
            pl.BlockSpec((tm, cw), lambda i, c: (i, c)),
        ],
        out_shape=[
            jax.ShapeDtypeStruct((n, d), BF16),
            jax.ShapeDtypeStruct((n, d), BF16),
            jax.ShapeDtypeStruct((n, d), BF16),
        ],
        scratch_shapes=[
            pltpu.VMEM((nch, SUBLANES, cw), F32),
            pltpu.VMEM((nch, SUBLANES, cw), F32),
            pltpu.VMEM((nch, SUBLANES, cw), F32),
            pltpu.VMEM((tm, cw), F32),
        ],
        compiler_params=pltpu.CompilerParams(
            dimension_semantics=("arbitrary", "arbitrary"), vmem_limit_bytes=VMEM_LIMIT),
        name="mix1",
    )(x2, gmix, w5, par, wgate)


def _mix2_kernel(x_ref, a_ref, yc_ref, hy_ref, wg2_ref, wa_ref, wb_ref, wo_ref,
                 gffn_ref, wrh_ref, wrl_ref, br_ref,
                 h_ref, rt_ref, cnt_ref,
                 cnt_s):
    i = pl.program_id(0)
    kc = pl.program_id(1)
    nkc = pl.num_programs(1)
    tm = x_ref.shape[0]
    cw = wa_ref.shape[2]

    @pl.when(jnp.logical_and(i == 0, kc == 0))
    def _():
        cnt_s[...] = jnp.zeros_like(cnt_s)

    g2 = _dot(a_ref[...], wg2_ref[0])
    y_conv = _dot(yc_ref[...], wa_ref[0])
    y_rec = _dot(hy_ref[...], wb_ref[0])
    merged = jax.nn.sigmoid(g2[:, :cw]) * y_conv + jax.nn.sigmoid(g2[:, cw:]) * y_rec
    contrib = _dot(merged.astype(BF16), wo_ref[0])

    @pl.when(kc == 0)
    def _():
        h_ref[...] = x_ref[...] + contrib

    @pl.when(kc > 0)
    def _():
        h_ref[...] += contrib

    @pl.when(kc == nkc - 1)
    def _():
        xn = _rms(h_ref[...], gffn_ref[...])
        xh = xn.astype(BF16)
        xl = (xn - xh.astype(F32)).astype(BF16)
        wrh = wrh_ref[...]
        logits = _dot(xh, wrh) + _dot(xh, wrl_ref[...]) + _dot(xl, wrh) + br_ref[...]
        lane = lax.broadcasted_iota(jnp.int32, (tm, LANES), 1)
        lane_f = lane.astype(F32)
        work = jnp.where(lane < N_EXPERTS, logits, NEG_BIG)
        vals, idxs, hots = [], [], []
        for _k in range(TOP_K):
            m = jnp.max(work, axis=-1, keepdims=True)
            idx = jnp.min(jnp.where(work == m, lane_f, float(LANES)), axis=-1, keepdims=True)
            hot = lane_f == idx
            work = jnp.where(hot, NEG_BIG, work)
            vals.append(m)
            idxs.append(idx)
            hots.append(hot)
        es = [jnp.exp(v - vals[0]) for v in vals]
        denom = es[0] + es[1] + es[2] + es[3]
        ws = [e / denom for e in es]
        hot_any = jnp.zeros((tm, LANES), F32)
        for hot in hots:
            hot_any = hot_any + hot.astype(F32)
        rr = lax.broadcasted_iota(jnp.int32, (tm, tm), 0)
        cc = lax.broadcasted_iota(jnp.int32, (tm, tm), 1)
        lower = jnp.where(rr > cc, 1.0, 0.0).astype(BF16)
        cum = _dot(lower, hot_any.astype(BF16)) + cnt_s[0:1, :]
        out = jnp.zeros((tm, LANES), F32)
        for k in range(TOP_K):
            rank = jnp.sum(jnp.where(hots[k], cum, 0.0), axis=-1, keepdims=True)
            out = jnp.where(lane == k, idxs[k], out)
            out = jnp.where(lane == TOP_K + k, ws[k], out)
            out = jnp.where(lane == 2 * TOP_K + k, rank, out)
        rt_ref[...] = out
        cnt_new = cnt_s[...] + jnp.sum(hot_any, axis=0, keepdims=True)
        cnt_s[...] = cnt_new
        cnt_ref[...] = cnt_new


def _mix2(x2, a, yc, hy, wg2, wa, wb, wo, gffn, wrh, wrl, br):
    n, d = x2.shape
    nkc, _, cw = wa.shape
    tm = TM_MIX
    grid = (n // tm, nkc)
    row = lambda i, k: (i, 0)
    const = lambda i, k: (0, 0)
    chunk = lambda i, k: (k, 0, 0)
    return pl.pallas_call(
        _mix2_kernel,
        grid=grid,
        in_specs=[
            pl.BlockSpec((tm, d), row),
            pl.BlockSpec((tm, d), row),
            pl.BlockSpec((tm, d), row),
            pl.BlockSpec((tm, d), row),
            pl.BlockSpec((1, d, 2 * cw), chunk),
            pl.BlockSpec((1, d, cw), chunk),
            pl.BlockSpec((1, d, cw), chunk),
            pl.BlockSpec((1, cw, d), chunk),
            pl.BlockSpec((1, d), const),
            pl.BlockSpec((d, LANES), const),
            pl.BlockSpec((d, LANES), const),
            pl.BlockSpec((1, LANES), const),
        ],
        out_specs=[
            pl.BlockSpec((tm, d), row),
            pl.BlockSpec((tm, LANES), row),
            pl.BlockSpec((SUBLANES, LANES), const),
        ],
        out_shape=[
            jax.ShapeDtypeStruct((n, d), F32),
            jax.ShapeDtypeStruct((n, LANES), F32),
            jax.ShapeDtypeStruct((SUBLANES, LANES), F32),
        ],
        scratch_shapes=[pltpu.VMEM((SUBLANES, LANES), F32)],
        compiler_params=pltpu.CompilerParams(
            dimension_semantics=("arbitrary", "arbitrary"), vmem_limit_bytes=VMEM_LIMIT),
        name="mix2",
    )(x2, a, yc, hy, wg2, wa, wb, wo, gffn, wrh, wrl, br)


def _moe_kernel(be_ref, nv_ref, tok_ref, h_hbm, gffn_ref, wg_ref, wu_ref, bg_ref, bu_ref, wd_ref,
                bd_ref, o_ref, xbuf, xb, sem):
    i = pl.program_id(0)
    j = pl.program_id(1)
    rows = xbuf.shape[0]
    valid = i < nv_ref[0]

    @pl.when(jnp.logical_and(valid, j == 0))
    def _():
        def issue(r, carry):
            t = tok_ref[0, 0, r]
            pltpu.make_async_copy(h_hbm.at[pl.ds(t, 1)], xbuf.at[pl.ds(r, 1)], sem).start()
            return carry

        lax.fori_loop(0, rows, issue, 0, unroll=8)
        pltpu.make_async_copy(h_hbm.at[pl.ds(0, rows)], xbuf, sem).wait()
        xb[...] = _rms(xbuf[...], gffn_ref[...]).astype(BF16)

    @pl.when(valid)
    def _():
        x = xb[...]
        gate = jnp.minimum(_dot(x, wg_ref[0]) + bg_ref[0], SWIGLU_LIMIT)
        up = jnp.clip(_dot(x, wu_ref[0]) + bu_ref[0], -SWIGLU_LIMIT, SWIGLU_LIMIT)
        glu = gate * jax.nn.sigmoid(SWIGLU_ALPHA * gate)
        act = ((up + 1.0) * glu).astype(BF16)
        contrib = _dot(act, wd_ref[0])

        @pl.when(j == 0)
        def _():
            o_ref[...] = contrib + bd_ref[0]

        @pl.when(j > 0)
        def _():
            o_ref[...] += contrib

    @pl.when(jnp.logical_and(jnp.logical_not(valid), j == 0))
    def _():
        o_ref[...] = jnp.zeros_like(o_ref)


def _moe(block_e, nvalid, slot_tok3, h, gffn, wgu, bgu3, wd, bd3):
    n, d = h.shape
    nb, _, rows = slot_tok3.shape
    f = wd.shape[1]
    nf = f // FC

    def jj(i, j, nv):
        return jnp.where(i < nv[0], j, nf - 1)

    grid_spec = pltpu.PrefetchScalarGridSpec(
        num_scalar_prefetch=2,
        grid=(nb, nf),
        in_specs=[
            pl.BlockSpec((1, 1, rows), lambda i, j, be, nv: (i, 0, 0), memory_space=pltpu.SMEM),
            pl.BlockSpec(memory_space=pl.ANY),
            pl.BlockSpec((1, d), lambda i, j, be, nv: (0, 0)),
            pl.BlockSpec((1, d, FC), lambda i, j, be, nv: (be[i], 0, jj(i, j, nv))),
            pl.BlockSpec((1, d, FC), lambda i, j, be, nv: (be[i], 0, nf + jj(i, j, nv))),
            pl.BlockSpec((1, 1, FC), lambda i, j, be, nv: (be[i], 0, jj(i, j, nv))),
            pl.BlockSpec((1, 1, FC), lambda i, j, be, nv: (be[i], 0, nf + jj(i, j, nv))),
            pl.BlockSpec((1, FC, d), lambda i, j, be, nv: (be[i], jj(i, j, nv), 0)),
            pl.BlockSpec((1, 1, d), lambda i, j, be, nv: (be[i], 0, 0)),
        ],
        out_specs=pl.BlockSpec((rows, d), lambda i, j, be, nv: (i, 0)),
        scratch_shapes=[
            pltpu.VMEM((rows, d), F32),
            pltpu.VMEM((rows, d), BF16),
            pltpu.SemaphoreType.DMA(()),
        ],
    )
    return pl.pallas_call(
        _moe_kernel,
        grid_spec=grid_spec,
        out_shape=jax.ShapeDtypeStruct((nb * rows, d), F32),
        compiler_params=pltpu.CompilerParams(
            dimension_semantics=("arbitrary", "arbitrary"), vmem_limit_bytes=VMEM_LIMIT),
        name="moe",
    )(block_e, nvalid, slot_tok3, h, gffn, wgu, wgu, bgu3, bgu3, wd, bd3)


def _combine_kernel(dst_ref, h_ref, ys_hbm, rt_ref, p_ref, gple_ref, wpg_ref, wpp_ref, gfin_ref,
                    o_ref, ybuf, sem):
    tm = h_ref.shape[0]

    def issue(r, carry):
        for k in range(TOP_K):
            s = dst_ref[0, 0, r * TOP_K + k]
            pltpu.make_async_copy(ys_hbm.at[pl.ds(s, 1)], ybuf.at[k, pl.ds(r, 1)], sem).start()
        return carry

    lax.fori_loop(0, tm, issue, 0, unroll=4)
    for k in range(TOP_K):
        pltpu.make_async_copy(ys_hbm.at[pl.ds(0, tm)], ybuf.at[k], sem).wait()

    rt = rt_ref[...]
    y = rt[:, TOP_K:TOP_K + 1] * ybuf[0]
    for k in range(1, TOP_K):
        y = y + rt[:, TOP_K + k:TOP_K + k + 1] * ybuf[k]
    h2 = h_ref[...] + y
    gate = jax.nn.sigmoid(_dot(_rms(h2, gple_ref[...]).astype(BF16), wpg_ref[...]))
    pp = _dot(p_ref[...].astype(BF16), wpp_ref[...])
    h3 = h2 + gate * pp
    o_ref[...] = _rms(h3, gfin_ref[...])


def _combine(dest3, h, ys, rt, p2, gple, wpg, wpp, gfin):
    n, d = h.shape
    tm = TM_OUT
    pd = p2.shape[1]
    row = lambda i: (i, 0)
    const = lambda i: (0, 0)
    return pl.pallas_call(
        _combine_kernel,
        grid=(n // tm,),
        in_specs=[
            pl.BlockSpec((1, 1, tm * TOP_K), lambda i: (i, 0, 0), memory_space=pltpu.SMEM),
            pl.BlockSpec((tm, d), row),
            pl.BlockSpec(memory_space=pl.ANY),
            pl.BlockSpec((tm, LANES), row),
            pl.BlockSpec((tm, pd), row),
            pl.BlockSpec((1, d), const),
            pl.BlockSpec((d, d), const),
            pl.BlockSpec((pd, d), const),
            pl.BlockSpec((1, d), const),
        ],
        out_specs=pl.BlockSpec((tm, d), row),
        out_shape=jax.ShapeDtypeStruct((n, d), F32),
        scratch_shapes=[
            pltpu.VMEM((TOP_K, tm, d), F32),
            pltpu.SemaphoreType.DMA(()),
        ],
        compiler_params=pltpu.CompilerParams(
            dimension_semantics=("arbitrary",), vmem_limit_bytes=VMEM_LIMIT),
        name="combine",
    )(dest3, h, ys, rt, p2, gple, wpg, wpp, gfin)


def _layer(h2, p2, norm_mix, w_in, conv_a_w, w_a_out, conv_b_w, conv_b_b, w_rg, b_rg, w_ig, b_ig,
           lru_lambda, w_b_out, w_o, norm_ffn, w_router, b_router, w_gu, b_gu, w_down, b_down,
           norm_ple, w_ple_gate, w_ple_proj, norm_out):
    n, d = h2.shape
    nch = LRU_HEADS
    cw = d // nch

    w5 = w_in[:, :5 * d].reshape(d, 5, nch, cw).transpose(2, 0, 1, 3).reshape(nch, d, 5 * cw).astype(BF16)
    wg2 = w_in[:, 5 * d:].reshape(d, 2, nch, cw).transpose(2, 0, 1, 3).reshape(nch, d, 2 * cw).astype(BF16)
    wa = w_a_out.reshape(d, nch, cw).transpose(1, 0, 2).astype(BF16)
    wb = w_b_out.reshape(d, nch, cw).transpose(1, 0, 2).astype(BF16)
    wo = w_o.reshape(nch, cw, d).astype(BF16)
    wgate = jnp.concatenate([w_rg, w_ig], axis=-1).astype(BF16)
    par = jnp.concatenate([conv_a_w, conv_b_w, conv_b_b[None], b_rg[None], b_ig[None],
                           lru_lambda[None], jnp.zeros((16 - 11, d), F32)], axis=0)
    wr_pad = jnp.pad(w_router, ((0, 0), (0, LANES - N_EXPERTS)))
    wrh = wr_pad.astype(BF16)
    wrl = (wr_pad - wrh.astype(F32)).astype(BF16)
    br = jnp.pad(b_router, (0, LANES - N_EXPERTS))[None]

    a, yc, hy = _mix1(h2, norm_mix[None], w5, par, wgate)
    h_mid, rt, cnt = _mix2(h2, a, yc, hy, wg2, wa, wb, wo, norm_ffn[None], wrh, wrl, br)

    top_idx = rt[:, 0:TOP_K].astype(jnp.int32)
    rank = rt[:, 2 * TOP_K:3 * TOP_K].astype(jnp.int32)
    counts = cnt[0, :N_EXPERTS].astype(jnp.int32)
    n_assign = n * TOP_K
    nb = -(-n_assign // BR) + N_EXPERTS
    padded = (counts + BR - 1) // BR * BR
    p_ends = jnp.cumsum(padded)
    p_starts = p_ends - padded
    dest = p_starts[top_idx] + rank
    tok = jnp.broadcast_to(jnp.arange(n, dtype=jnp.int32)[:, None], (n, TOP_K))
    slot_tok = jnp.zeros((nb * BR,), jnp.int32).at[dest.reshape(-1)].set(tok.reshape(-1))
    nvalid = (p_ends[-1] // BR).astype(jnp.int32)
    blk = jnp.arange(nb, dtype=jnp.int32)
    block_e = jnp.minimum(jnp.searchsorted(p_ends, blk * BR, side="right"), N_EXPERTS - 1).astype(jnp.int32)
    block_e = jnp.where(blk < nvalid, block_e, block_e[nvalid - 1])

    ys = _moe(block_e, nvalid[None], slot_tok.reshape(nb, 1, BR), h_mid, norm_ffn[None],
              w_gu.astype(BF16), b_gu[:, None, :], w_down.astype(BF16), b_down[:, None, :])

    out = _combine(dest.reshape(n // TM_OUT, 1, TM_OUT * TOP_K), h_mid, ys, rt, p2, norm_ple[None],
                   w_ple_gate.astype(BF16), w_ple_proj.astype(BF16), norm_out[None])
    return out


def kernel(x, p, norm_mix, w_in, conv_a_w, w_a_out, conv_b_w, conv_b_b, w_rg, b_rg, w_ig, b_ig, lru_lambda, w_b_out, w_o, norm_ffn, w_router, b_router, w_gu, b_gu, w_down, b_down, norm_ple, w_ple_gate, w_ple_proj, norm_final):
    bsz, t, d = x.shape
    depth = p.shape[0]
    assert bsz == 1 and depth == 1, "kernel fuses the final norm into the single layer"
    out = _layer(x.reshape(t, d), p[0].reshape(t, -1), norm_mix[0], w_in[0], conv_a_w[0], w_a_out[0],
                 conv_b_w[0], conv_b_b[0], w_rg[0], b_rg[0], w_ig[0], b_ig[0], lru_lambda[0], w_b_out[0],
                 w_o[0], norm_ffn[0], w_router[0], b_router[0], w_gu[0], b_gu[0], w_down[0], b_down[0],
                 norm_ple[0], w_ple_gate[0], w_ple_proj[0], norm_final)
    return out.reshape(bsz, t, d)
```

```python
import functools

import jax
import jax.numpy as jnp
from jax import lax
from jax.experimental import pallas as pl
from jax.experimental.pallas import tpu as pltpu

F32 = jnp.float32
BF16 = jnp.bfloat16

NORM_EPS = 1e-6
LRU_HEADS = 8
LRU_C = 8.0
N_EXPERTS = 32
TOP_K = 4
SWIGLU_LIMIT = 7.0
SWIGLU_ALPHA = 1.702

LANES = 128
SUBLANES = 8
VMEM_LIMIT = 56 * 1024 * 1024

TM_MIX = 512
TM_OUT = 256
TR = 1024
FC = 256
NEG_BIG = -3.0e38


def _rms(x, g):
    ms = jnp.mean(x * x, axis=-1, keepdims=True)
    return x * lax.rsqrt(ms + NORM_EPS) * g


def _dot(a, b):
    return jnp.dot(a, b, preferred_element_type=F32)


def _shift_rows(u, hist8, k):
    rolled = pltpu.roll(u, k, axis=0)
    hr = pltpu.roll(hist8, k, axis=0)
    row = lax.broadcasted_iota(jnp.int32, hist8.shape, 0)
    first = jnp.where(row < k, hr, rolled[:SUBLANES])
    return jnp.concatenate([first, rolled[SUBLANES:]], axis=0)


def _gelu_tanh(x):
    return 0.5 * x * (1.0 + jnp.tanh(0.7978845608028654 * (x + 0.044715 * (x * x * x))))


def _mix1_kernel(x_ref, gmix_ref, w5_ref, par_ref, wgate_ref,
                 a_ref, yc_ref, hy_ref,
                 hu_s, hx_s, hp_s, h_s):
    i = pl.program_id(0)
    c = pl.program_id(1)
    tm, cw = yc_ref.shape

    @pl.when(jnp.logical_and(i == 0, c == 0))
    def _():
        hu_s[...] = jnp.zeros_like(hu_s)
        hx_s[...] = jnp.zeros_like(hx_s)
        hp_s[...] = jnp.zeros_like(hp_s)

    @pl.when(c == 0)
    def _():
        a_ref[...] = _rms(x_ref[...], gmix_ref[...]).astype(BF16)

    proj = _dot(a_ref[...], w5_ref[0])
    b_c = proj[:, 0 * cw:1 * cw]
    c_c = proj[:, 1 * cw:2 * cw]
    v_c = proj[:, 2 * cw:3 * cw]
    y_r = proj[:, 3 * cw:4 * cw]
    x_r = proj[:, 4 * cw:5 * cw]
    par = par_ref[...]

    u = c_c * v_c
    hu = hu_s[c]
    conv = (par[0:1] * _shift_rows(u, hu, 2) + par[1:2] * _shift_rows(u, hu, 1) + par[2:3] * u)
    yc_ref[...] = (b_c * conv).astype(BF16)
    hu_s[c] = u[tm - SUBLANES:, :]

    hx = hx_s[c]
    xc = (par[3:4] * _shift_rows(x_r, hx, 3) + par[4:5] * _shift_rows(x_r, hx, 2)
          + par[5:6] * _shift_rows(x_r, hx, 1) + par[6:7] * x_r) + par[7:8]
    hx_s[c] = x_r[tm - SUBLANES:, :]
    gz = _dot(xc.astype(BF16), wgate_ref[0])
    r = jax.nn.sigmoid(gz[:, :cw] + par[8:9])
    ig = jax.nn.sigmoid(gz[:, cw:] + par[9:10])
    nlam = -par[10:11]
    softplus = jnp.maximum(nlam, 0.0) + jnp.log1p(jnp.exp(-jnp.abs(nlam)))
    log_a = (-LRU_C) * r * softplus
    a = jnp.exp(log_a)
    mult = jnp.sqrt(1.0 - a * a)
    trow = lax.broadcasted_iota(jnp.int32, (tm, cw), 0) + i * tm
    mult = jnp.where(trow == 0, 1.0, mult)
    b = xc * ig * mult

    sub = lax.broadcasted_iota(jnp.int32, (tm, cw), 0) % SUBLANES
    for s in (1, 2, 4):
        a_sh = pltpu.roll(a, s, axis=0)
        b_sh = pltpu.roll(b, s, axis=0)
        m = sub >= s
        b = jnp.where(m, a * b_sh + b, b)
        a = jnp.where(m, a * a_sh, a)
    carry = hp_s[c][SUBLANES - 1:SUBLANES, :]
    for g in range(tm // SUBLANES):
        lo = g * SUBLANES
        hg = b[lo:lo + SUBLANES, :] + a[lo:lo + SUBLANES, :] * carry
        h_s[lo:lo + SUBLANES, :] = hg
        carry = hg[SUBLANES - 1:SUBLANES, :]
    hp_s[c] = h_s[tm - SUBLANES:, :]
    hy_ref[...] = (h_s[...] * _gelu_tanh(y_r)).astype(BF16)


def _mix1(x2, gmix, w5, par, wgate):
    n, d = x2.shape
    nch, _, w5c = w5.shape
    cw = w5c // 5
    tm = TM_MIX
    grid = (n // tm, nch)
    return pl.pallas_call(
        _mix1_kernel,
        grid=grid,
        in_specs=[
            pl.BlockSpec((tm, d), lambda i, c: (i, 0)),
            pl.BlockSpec((1, d), lambda i, c: (0, 0)),
            pl.BlockSpec((1, d, w5c), lambda i, c: (c, 0, 0)),
            pl.BlockSpec((16, cw), lambda i, c: (0, c)),
            pl.BlockSpec((1, cw, 2 * cw), lambda i, c: (c, 0, 0)),
        ],
        out_specs=[
            pl.BlockSpec((tm, d), lambda i, c: (i, 0)),
            pl.BlockSpec((tm, cw), lambda i, c: (i, c)),
            pl.BlockSpec((tm, cw), lambda i, c: (i, c)),
        ],
        out_shape=[
            jax.ShapeDtypeStruct((n, d), BF16),
            jax.ShapeDtypeStruct((n, d), BF16),
            jax.ShapeDtypeStruct((n, d), BF16),
        ],
        scratch_shapes=[
            pltpu.VMEM((nch, SUBLANES, cw), F32),
            pltpu.VMEM((nch, SUBLANES, cw), F32),
            pltpu.VMEM((nch, SUBLANES, cw), F32),
            pltpu.VMEM((tm, cw), F32),
        ],
        compiler_params=pltpu.CompilerParams(
            dimension_semantics=("arbitrary", "arbitrary"), vmem_limit_bytes=VMEM_LIMIT),
        name="mix1",
    )(x2, gmix, w5, par, wgate)


def _mix2_kernel(x_ref, a_ref, yc_ref, hy_ref, wg2_ref, wa_ref, wb_ref, wo_ref,
                 gffn_ref, wrh_ref, wrl_ref, br_ref,
                 h_ref, xq_ref, rt_ref, cnt_ref,
                 cnt_s):
    i = pl.program_id(0)
    kc = pl.program_id(1)
    nkc = pl.num_programs(1)
    tm = x_ref.shape[0]
    cw = wa_ref.shape[2]

    @pl.when(jnp.logical_and(i == 0, kc == 0))
    def _():
        cnt_s[...] = jnp.zeros_like(cnt_s)

    g2 = _dot(a_ref[...], wg2_ref[0])
    y_conv = _dot(yc_ref[...], wa_ref[0])
    y_rec = _dot(hy_ref[...], wb_ref[0])
    merged = jax.nn.sigmoid(g2[:, :cw]) * y_conv + jax.nn.sigmoid(g2[:, cw:]) * y_rec
    contrib = _dot(merged.astype(BF16), wo_ref[0])

    @pl.when(kc == 0)
    def _():
        h_ref[...] = x_ref[...] + contrib

    @pl.when(kc > 0)
    def _():
        h_ref[...] += contrib

    @pl.when(kc == nkc - 1)
    def _():
        xn = _rms(h_ref[...], gffn_ref[...])
        xh = xn.astype(BF16)
        xhf = xh.astype(F32)
        xl = (xn - xhf).astype(BF16)
        half = xn.shape[1] // 2
        lo = pltpu.bitcast(xhf[:, :half], jnp.uint32) >> 16
        hi = pltpu.bitcast(xhf[:, half:], jnp.uint32) & jnp.uint32(0xFFFF0000)
        xq_ref[...] = lo | hi
        wrh = wrh_ref[...]
        logits = _dot(xh, wrh) + _dot(xh, wrl_ref[...]) + _dot(xl, wrh) + br_ref[...]
        lane = lax.broadcasted_iota(jnp.int32, (tm, LANES), 1)
        lane_f = lane.astype(F32)
        work = jnp.where(lane < N_EXPERTS, logits, NEG_BIG)
        vals, idxs, hots = [], [], []
        for _k in range(TOP_K):
            m = jnp.max(work, axis=-1, keepdims=True)
            idx = jnp.min(jnp.where(work == m, lane_f, float(LANES)), axis=-1, keepdims=True)
            hot = lane_f == idx
            work = jnp.where(hot, NEG_BIG, work)
            vals.append(m)
            idxs.append(idx)
            hots.append(hot)
        es = [jnp.exp(v - vals[0]) for v in vals]
        denom = es[0] + es[1] + es[2] + es[3]
        ws = [e / denom for e in es]
        hot_any = jnp.zeros((tm, LANES), F32)
        for hot in hots:
            hot_any = hot_any + hot.astype(F32)
        rr = lax.broadcasted_iota(jnp.int32, (tm, tm), 0)
        cc = lax.broadcasted_iota(jnp.int32, (tm, tm), 1)
        lower = jnp.where(rr > cc, 1.0, 0.0).astype(BF16)
        cum = _dot(lower, hot_any.astype(BF16)) + cnt_s[0:1, :]
        out = jnp.zeros((tm, LANES), F32)
        for k in range(TOP_K):
            rank = jnp.sum(jnp.where(hots[k], cum, 0.0), axis=-1, keepdims=True)
            out = jnp.where(lane == k, idxs[k], out)
            out = jnp.where(lane == TOP_K + k, ws[k], out)
            out = jnp.where(lane == 2 * TOP_K + k, rank, out)
        rt_ref[...] = out
        cnt_new = cnt_s[...] + jnp.sum(hot_any, axis=0, keepdims=True)
        cnt_s[...] = cnt_new
        cnt_ref[...] = cnt_new


def _mix2(x2, a, yc, hy, wg2, wa, wb, wo, gffn, wrh, wrl, br):
    n, d = x2.shape
    nkc, _, cw = wa.shape
    tm = TM_MIX
    grid = (n // tm, nkc)
    row = lambda i, k: (i, 0)
    const = lambda i, k: (0, 0)
    chunk = lambda i, k: (k, 0, 0)
    return pl.pallas_call(
        _mix2_kernel,
        grid=grid,
        in_specs=[
            pl.BlockSpec((tm, d), row),
            pl.BlockSpec((tm, d), row),
            pl.BlockSpec((tm, d), row),
            pl.BlockSpec((tm, d), row),
            pl.BlockSpec((1, d, 2 * cw), chunk),
            pl.BlockSpec((1, d, cw), chunk),
            pl.BlockSpec((1, d, cw), chunk),
            pl.BlockSpec((1, cw, d), chunk),
            pl.BlockSpec((1, d), const),
            pl.BlockSpec((d, LANES), const),
            pl.BlockSpec((d, LANES), const),
            pl.BlockSpec((1, LANES), const),
        ],
        out_specs=[
            pl.BlockSpec((tm, d), row),
            pl.BlockSpec((tm, d // 2), row),
            pl.BlockSpec((tm, LANES), row),
            pl.BlockSpec((SUBLANES, LANES), const),
        ],
        out_shape=[
            jax.ShapeDtypeStruct((n, d), F32),
            jax.ShapeDtypeStruct((n, d // 2), jnp.uint32),
            jax.ShapeDtypeStruct((n, LANES), F32),
            jax.ShapeDtypeStruct((SUBLANES, LANES), F32),
        ],
        scratch_shapes=[pltpu.VMEM((SUBLANES, LANES), F32)],
        compiler_params=pltpu.CompilerParams(
            dimension_semantics=("arbitrary", "arbitrary"), vmem_limit_bytes=VMEM_LIMIT),
        name="mix2",
    )(x2, a, yc, hy, wg2, wa, wb, wo, gffn, wrh, wrl, br)


def _moe_kernel(te_ref, ns_ref,
                tok_ref, xq_hbm, wg_ref, wu_ref, bg_ref, bu_ref, wd_ref, bd_ref,
                o_ref, gbuf, xb, wgu_b, wd_b, sem):
    del te_ref
    i = pl.program_id(0)
    j = pl.program_id(1)
    nf = pl.num_programs(1)
    nt = pl.num_programs(0) - 1
    tr = o_ref.shape[0]
    sb = tr // 2
    fc = wg_ref.shape[2]
    half = xq_hbm.shape[1]
    per_step = tr // nf

    ns_next = jnp.where(i < nt, ns_ref[jnp.minimum(i, nt - 1)], 0)
    slot_n = i % 2

    @pl.when(j * per_step < ns_next * sb)
    def _():
        base = j * per_step

        def issue(r, carry):
            t = tok_ref[0, 0, base + r]
            pltpu.make_async_copy(xq_hbm.at[pl.ds(t, 1)], gbuf.at[slot_n, pl.ds(base + r, 1)],
                                  sem.at[slot_n]).start()
            return carry

        lax.fori_loop(0, per_step, issue, 0, unroll=8)

    ic = jnp.maximum(i - 1, 0)
    ns_cur = jnp.where(i > 0, ns_ref[ic], 0)
    slot_c = ic % 2

    @pl.when(jnp.logical_and(ns_cur > 0, j == 0))
    def _():
        rows = ns_cur * sb
        pltpu.make_async_copy(xq_hbm.at[pl.ds(0, rows)], gbuf.at[slot_c, pl.ds(0, rows)],
                              sem.at[slot_c]).wait()

    def unpack(sub):
        w = gbuf[slot_c, sub * sb:(sub + 1) * sb, :]
        lo = pltpu.bitcast(w << 16, F32).astype(BF16)
        hi = pltpu.bitcast(w & jnp.uint32(0xFFFF0000), F32).astype(BF16)
        xb[sub * sb:(sub + 1) * sb, :half] = lo
        xb[sub * sb:(sub + 1) * sb, half:] = hi

    @pl.when(ns_cur > 0)
    def _():
        wgu_b[:, :fc] = wg_ref[0].astype(BF16)
        wgu_b[:, fc:] = wu_ref[0].astype(BF16)
        wd_b[...] = wd_ref[0].astype(BF16)

    def expert_rows(sub):
        @pl.when(j == 0)
        def _():
            unpack(sub)

        x = xb[sub * sb:(sub + 1) * sb, :]
        gu = _dot(x, wgu_b[...])
        gate = jnp.minimum(gu[:, :fc] + bg_ref[0], SWIGLU_LIMIT)
        up = jnp.clip(gu[:, fc:] + bu_ref[0], -SWIGLU_LIMIT, SWIGLU_LIMIT)
        glu = gate * jax.nn.sigmoid(SWIGLU_ALPHA * gate)
        act = ((up + 1.0) * glu).astype(BF16)
        contrib = _dot(act, wd_b[...])

        @pl.when(j == 0)
        def _():
            o_ref[sub * sb:(sub + 1) * sb, :] = contrib + bd_ref[0]

        @pl.when(j > 0)
        def _():
            o_ref[sub * sb:(sub + 1) * sb, :] += contrib

    for sub in range(2):
        @pl.when(ns_cur > sub)
        def _(sub=sub):
            expert_rows(sub)

        @pl.when(jnp.logical_and(ns_cur <= sub, j == 0))
        def _(sub=sub):
            o_ref[sub * sb:(sub + 1) * sb, :] = jnp.zeros((sb, o_ref.shape[1]), o_ref.dtype)


def _moe(tile_e, tile_ns, slot_tok3, xq, wgu, bgu3, wd, bd3):
    n, half = xq.shape
    d = 2 * half
    nt, _, tr = slot_tok3.shape
    f = wd.shape[1]
    nf = f // FC

    def cur(i):
        return jnp.maximum(i - 1, 0)

    def jj(i, j, ns):
        return jnp.where(i == 0, 0, jnp.where(ns[cur(i)] > 0, j, nf - 1))

    grid_spec = pltpu.PrefetchScalarGridSpec(
        num_scalar_prefetch=2,
        grid=(nt + 1, nf),
        in_specs=[
            pl.BlockSpec((1, 1, tr), lambda i, j, te, ns: (jnp.minimum(i, nt - 1), 0, 0),
                         memory_space=pltpu.SMEM),
            pl.BlockSpec(memory_space=pl.ANY),
            pl.BlockSpec((1, d, FC), lambda i, j, te, ns: (te[cur(i)], 0, jj(i, j, ns))),
            pl.BlockSpec((1, d, FC), lambda i, j, te, ns: (te[cur(i)], 0, nf + jj(i, j, ns))),
            pl.BlockSpec((1, 1, FC), lambda i, j, te, ns: (te[cur(i)], 0, jj(i, j, ns))),
            pl.BlockSpec((1, 1, FC), lambda i, j, te, ns: (te[cur(i)], 0, nf + jj(i, j, ns))),
            pl.BlockSpec((1, FC, d), lambda i, j, te, ns: (te[cur(i)], jj(i, j, ns), 0)),
            pl.BlockSpec((1, 1, d), lambda i, j, te, ns: (te[cur(i)], 0, 0)),
        ],
        out_specs=pl.BlockSpec((tr, d), lambda i, j, te, ns: (cur(i), 0)),
        scratch_shapes=[
            pltpu.VMEM((2, tr, half), jnp.uint32),
            pltpu.VMEM((tr, d), BF16),
            pltpu.VMEM((d, 2 * FC), BF16),
            pltpu.VMEM((FC, d), BF16),
            pltpu.SemaphoreType.DMA((2,)),
        ],
    )
    return pl.pallas_call(
        _moe_kernel,
        grid_spec=grid_spec,
        out_shape=jax.ShapeDtypeStruct((nt * tr, d), F32),
        compiler_params=pltpu.CompilerParams(
            dimension_semantics=("arbitrary", "arbitrary"), vmem_limit_bytes=VMEM_LIMIT),
        name="moe",
    )(tile_e, tile_ns, slot_tok3, xq, wgu, wgu, bgu3, bgu3, wd, bd3)


def _combine_kernel(dst_ref, h_ref, ys_hbm, rt_ref, p_ref, gple_ref, wpg_ref, wpp_ref, gfin_ref,
                    o_ref, ybuf, sem):
    tm = h_ref.shape[0]

    def issue(r, carry):
        for k in range(TOP_K):
            s = dst_ref[0, 0, r * TOP_K + k]
            pltpu.make_async_copy(ys_hbm.at[pl.ds(s, 1)], ybuf.at[k, pl.ds(r, 1)], sem).start()
        return carry

    lax.fori_loop(0, tm, issue, 0, unroll=4)
    for k in range(TOP_K):
        pltpu.make_async_copy(ys_hbm.at[pl.ds(0, tm)], ybuf.at[k], sem).wait()

    rt = rt_ref[...]
    y = rt[:, TOP_K:TOP_K + 1] * ybuf[0]
    for k in range(1, TOP_K):
        y = y + rt[:, TOP_K + k:TOP_K + k + 1] * ybuf[k]
    h2 = h_ref[...] + y
    gate = jax.nn.sigmoid(_dot(_rms(h2, gple_ref[...]).astype(BF16), wpg_ref[...]))
    pp = _dot(p_ref[...].astype(BF16), wpp_ref[...])
    h3 = h2 + gate * pp
    o_ref[...] = _rms(h3, gfin_ref[...])


def _combine(dest3, h, ys, rt, p2, gple, wpg, wpp, gfin):
    n, d = h.shape
    tm = TM_OUT
    pd = p2.shape[1]
    row = lambda i: (i, 0)
    const = lambda i: (0, 0)
    return pl.pallas_call(
        _combine_kernel,
        grid=(n // tm,),
        in_specs=[
            pl.BlockSpec((1, 1, tm * TOP_K), lambda i: (i, 0, 0), memory_space=pltpu.SMEM),
            pl.BlockSpec((tm, d), row),
            pl.BlockSpec(memory_space=pl.ANY),
            pl.BlockSpec((tm, LANES), row),
            pl.BlockSpec((tm, pd), row),
            pl.BlockSpec((1, d), const),
            pl.BlockSpec((d, d), const),
            pl.BlockSpec((pd, d), const),
            pl.BlockSpec((1, d), const),
        ],
        out_specs=pl.BlockSpec((tm, d), row),
        out_shape=jax.ShapeDtypeStruct((n, d), F32),
        scratch_shapes=[
            pltpu.VMEM((TOP_K, tm, d), F32),
            pltpu.SemaphoreType.DMA(()),
        ],
        compiler_params=pltpu.CompilerParams(
            dimension_semantics=("arbitrary",), vmem_limit_bytes=VMEM_LIMIT),
        name="combine",
    )(dest3, h, ys, rt, p2, gple, wpg, wpp, gfin)


def _layer(h2, p2, norm_mix, w_in, conv_a_w, w_a_out, conv_b_w, conv_b_b, w_rg, b_rg, w_ig, b_ig,
           lru_lambda, w_b_out, w_o, norm_ffn, w_router, b_router, w_gu, b_gu, w_down, b_down,
           norm_ple, w_ple_gate, w_ple_proj, norm_out):
    n, d = h2.shape
    nch = LRU_HEADS
    cw = d // nch

    w5 = w_in[:, :5 * d].reshape(d, 5, nch, cw).transpose(2, 0, 1, 3).reshape(nch, d, 5 * cw).astype(BF16)
    wg2 = w_in[:, 5 * d:].reshape(d, 2, nch, cw).transpose(2, 0, 1, 3).reshape(nch, d, 2 * cw).astype(BF16)
    wa = w_a_out.reshape(d, nch, cw).transpose(1, 0, 2).astype(BF16)
    wb = w_b_out.reshape(d, nch, cw).transpose(1, 0, 2).astype(BF16)
    wo = w_o.reshape(nch, cw, d).astype(BF16)
    wgate = jnp.concatenate([w_rg, w_ig], axis=-1).astype(BF16)
    par = jnp.concatenate([conv_a_w, conv_b_w, conv_b_b[None], b_rg[None], b_ig[None],
                           lru_lambda[None], jnp.zeros((16 - 11, d), F32)], axis=0)
    wr_pad = jnp.pad(w_router, ((0, 0), (0, LANES - N_EXPERTS)))
    wrh = wr_pad.astype(BF16)
    wrl = (wr_pad - wrh.astype(F32)).astype(BF16)
    br = jnp.pad(b_router, (0, LANES - N_EXPERTS))[None]

    a, yc, hy = _mix1(h2, norm_mix[None], w5, par, wgate)
    h_mid, xq, rt, cnt = _mix2(h2, a, yc, hy, wg2, wa, wb, wo, norm_ffn[None], wrh, wrl, br)

    top_idx = rt[:, 0:TOP_K].astype(jnp.int32)
    rank = rt[:, 2 * TOP_K:3 * TOP_K].astype(jnp.int32)
    counts = cnt[0, :N_EXPERTS].astype(jnp.int32)
    sb = TR // 2
    nt = -(-(n * TOP_K) // TR) + N_EXPERTS
    n_sub = (counts + sb - 1) // sb
    n_tile = (n_sub + 1) // 2
    t_ends = jnp.cumsum(n_tile)
    t_starts = t_ends - n_tile
    n_tiles = t_ends[-1]
    dest = (t_starts * TR)[top_idx] + rank
    tok = jnp.broadcast_to(jnp.arange(n, dtype=jnp.int32)[:, None], (n, TOP_K))
    slot_tok = jnp.zeros((nt * TR,), jnp.int32).at[dest.reshape(-1)].set(tok.reshape(-1))
    ti = jnp.arange(nt, dtype=jnp.int32)
    e_of = jnp.minimum(jnp.sum((t_ends[None, :] <= ti[:, None]).astype(jnp.int32), axis=1), N_EXPERTS - 1)
    tile_ns = jnp.where(ti < n_tiles, jnp.clip(n_sub[e_of] - 2 * (ti - t_starts[e_of]), 0, 2), 0)
    tile_e = jnp.where(ti < n_tiles, e_of, e_of[n_tiles - 1])

    ys = _moe(tile_e.astype(jnp.int32), tile_ns.astype(jnp.int32), slot_tok.reshape(nt, 1, TR), xq,
              w_gu, b_gu[:, None, :], w_down, b_down[:, None, :])

    out = _combine(dest.reshape(n // TM_OUT, 1, TM_OUT * TOP_K), h_mid, ys, rt, p2, norm_ple[None],
                   w_ple_gate.astype(BF16), w_ple_proj.astype(BF16), norm_out[None])
    return out


def kernel(x, p, norm_mix, w_in, conv_a_w, w_a_out, conv_b_w, conv_b_b, w_rg, b_rg, w_ig, b_ig, lru_lambda, w_b_out, w_o, norm_ffn, w_router, b_router, w_gu, b_gu, w_down, b_down, norm_ple, w_ple_gate, w_ple_proj, norm_final):
    bsz, t, d = x.shape
    depth = p.shape[0]
    assert bsz == 1 and depth == 1, "kernel fuses the final norm into the single layer"
    out = _layer(x.reshape(t, d), p[0].reshape(t, -1), norm_mix[0], w_in[0], conv_a_w[0], w_a_out[0],
                 conv_b_w[0], conv_b_b[0], w_rg[0], b_rg[0], w_ig[0], b_ig[0], lru_lambda[0], w_b_out[0],
                 w_o[0], norm_ffn[0], w_router[0], b_router[0], w_gu[0], b_gu[0], w_down[0], b_down[0],
                 norm_ple[0], w_ple_gate[0], w_ple_proj[0], norm_final)
    return out.reshape(bsz, t, d)
```

```python
import functools

import jax
import jax.numpy as jnp
from jax import lax
from jax.experimental import pallas as pl
from jax.experimental.pallas import tpu as pltpu

F32 = jnp.float32
BF16 = jnp.bfloat16

NORM_EPS = 1e-6
LRU_HEADS = 8
LRU_C = 8.0
N_EXPERTS = 32
TOP_K = 4
SWIGLU_LIMIT = 7.0
SWIGLU_ALPHA = 1.702

LANES = 128
SUBLANES = 8
VMEM_LIMIT = 56 * 1024 * 1024

TM_MIX = 512
TM_OUT = 256
TR = 1024
FC = 256
MOE_CHAIN = 256
MIX2_CHAIN = 256
NEG_BIG = -3.0e38


def _rms(x, g):
    ms = jnp.mean(x * x, axis=-1, keepdims=True)
    return x * lax.rsqrt(ms + NORM_EPS) * g


def _dot(a, b):
    return jnp.dot(a, b, preferred_element_type=F32)


def _shift_rows(u, hist8, k):
    rolled = pltpu.roll(u, k, axis=0)
    hr = pltpu.roll(hist8, k, axis=0)
    row = lax.broadcasted_iota(jnp.int32, hist8.shape, 0)
    first = jnp.where(row < k, hr, rolled[:SUBLANES])
    return jnp.concatenate([first, rolled[SUBLANES:]], axis=0)


def _gelu_tanh(x):
    return 0.5 * x * (1.0 + jnp.tanh(0.7978845608028654 * (x + 0.044715 * (x * x * x))))


def _mix1_kernel(x_ref, gmix_ref, w5_ref, par_ref, wgate_ref,
                 a_ref, yc_ref, hy_ref,
                 hu_s, hx_s, hp_s, h_s):
    i = pl.program_id(0)
    c = pl.program_id(1)
    tm, cw = yc_ref.shape

    @pl.when(jnp.logical_and(i == 0, c == 0))
    def _():
        hu_s[...] = jnp.zeros_like(hu_s)
        hx_s[...] = jnp.zeros_like(hx_s)
        hp_s[...] = jnp.zeros_like(hp_s)

    @pl.when(c == 0)
    def _():
        a_ref[...] = _rms(x_ref[...], gmix_ref[...]).astype(BF16)

    proj = _dot(a_ref[...], w5_ref[0])
    b_c = proj[:, 0 * cw:1 * cw]
    c_c = proj[:, 1 * cw:2 * cw]
    v_c = proj[:, 2 * cw:3 * cw]
    y_r = proj[:, 3 * cw:4 * cw]
    x_r = proj[:, 4 * cw:5 * cw]
    par = par_ref[...]

    u = c_c * v_c
    hu = hu_s[c]
    conv = (par[0:1] * _shift_rows(u, hu, 2) + par[1:2] * _shift_rows(u, hu, 1) + par[2:3] * u)
    yc_ref[...] = (b_c * conv).astype(BF16)
    hu_s[c] = u[tm - SUBLANES:, :]

    hx = hx_s[c]
    xc = (par[3:4] * _shift_rows(x_r, hx, 3) + par[4:5] * _shift_rows(x_r, hx, 2)
          + par[5:6] * _shift_rows(x_r, hx, 1) + par[6:7] * x_r) + par[7:8]
    hx_s[c] = x_r[tm - SUBLANES:, :]
    gz = _dot(xc.astype(BF16), wgate_ref[0])
    r = jax.nn.sigmoid(gz[:, :cw] + par[8:9])
    ig = jax.nn.sigmoid(gz[:, cw:] + par[9:10])
    nlam = -par[10:11]
    softplus = jnp.maximum(nlam, 0.0) + jnp.log1p(jnp.exp(-jnp.abs(nlam)))
    log_a = (-LRU_C) * r * softplus
    a = jnp.exp(log_a)
    mult = jnp.sqrt(1.0 - a * a)
    trow = lax.broadcasted_iota(jnp.int32, (tm, cw), 0) + i * tm
    mult = jnp.where(trow == 0, 1.0, mult)
    b = xc * ig * mult

    sub = lax.broadcasted_iota(jnp.int32, (tm, cw), 0) % SUBLANES
    for s in (1, 2, 4):
        a_sh = pltpu.roll(a, s, axis=0)
        b_sh = pltpu.roll(b, s, axis=0)
        m = sub >= s
        b = jnp.where(m, a * b_sh + b, b)
        a = jnp.where(m, a * a_sh, a)
    carry = hp_s[c][SUBLANES - 1:SUBLANES, :]
    for g in range(tm // SUBLANES):
        lo = g * SUBLANES
        hg = b[lo:lo + SUBLANES, :] + a[lo:lo + SUBLANES, :] * carry
        h_s[lo:lo + SUBLANES, :] = hg
        carry = hg[SUBLANES - 1:SUBLANES, :]
    hp_s[c] = h_s[tm - SUBLANES:, :]
    hy_ref[...] = (h_s[...] * _gelu_tanh(y_r)).astype(BF16)


def _mix1(x2, gmix, w5, par, wgate):
    n, d = x2.shape
    nch, _, w5c = w5.shape
    cw = w5c // 5
    tm = TM_MIX
    grid = (n // tm, nch)
    return pl.pallas_call(
        _mix1_kernel,
        grid=grid,
        in_specs=[
            pl.BlockSpec((tm, d), lambda i, c: (i, 0)),
            pl.BlockSpec((1, d), lambda i, c: (0, 0)),
            pl.BlockSpec((1, d, w5c), lambda i, c: (c, 0, 0)),
            pl.BlockSpec((16, cw), lambda i, c: (0, c)),
            pl.BlockSpec((1, cw, 2 * cw), lambda i, c: (c, 0, 0)),
        ],
        out_specs=[
            pl.BlockSpec((tm, d), lambda i, c: (i, 0)),
            pl.BlockSpec((tm, cw), lambda i, c: (i, c)),
            pl.BlockSpec((tm, cw), lambda i, c: (i, c)),
        ],
        out_shape=[
            jax.ShapeDtypeStruct((n, d), BF16),
            jax.ShapeDtypeStruct((n, d), BF16),
            jax.ShapeDtypeStruct((n, d), BF16),
        ],
        scratch_shapes=[
            pltpu.VMEM((nch, SUBLANES, cw), F32),
            pltpu.VMEM((nch, SUBLANES, cw), F32),
            pltpu.VMEM((nch, SUBLANES, cw), F32),
            pltpu.VMEM((tm, cw), F32),
        ],
        compiler_params=pltpu.CompilerParams(
            dimension_semantics=("arbitrary", "arbitrary"), vmem_limit_bytes=VMEM_LIMIT),
        name="mix1",
    )(x2, gmix, w5, par, wgate)


def _mix2_kernel(x_ref, a_ref, yc_ref, hy_ref, wg2_ref, wa_ref, wb_ref, wo_ref,
                 gffn_ref, wrh_ref, wrl_ref, br_ref,
                 h_ref, xq_ref, rt_ref, cnt_ref,
                 cnt_s):
    i = pl.program_id(0)
    kc = pl.program_id(1)
    nkc = pl.num_programs(1)
    tm = x_ref.shape[0]
    cw = wa_ref.shape[2]

    @pl.when(jnp.logical_and(i == 0, kc == 0))
    def _():
        cnt_s[...] = jnp.zeros_like(cnt_s)

    @pl.when(kc == 0)
    def _():
        h_ref[...] = x_ref[...]

    wg2 = wg2_ref[0]
    wa = wa_ref[0]
    wb = wb_ref[0]
    wo = wo_ref[0]
    for c in range(tm // MIX2_CHAIN):
        rs = slice(c * MIX2_CHAIN, (c + 1) * MIX2_CHAIN)
        g2 = _dot(a_ref[rs, :], wg2)
        y_conv = _dot(yc_ref[rs, :], wa)
        y_rec = _dot(hy_ref[rs, :], wb)
        merged = jax.nn.sigmoid(g2[:, :cw]) * y_conv + jax.nn.sigmoid(g2[:, cw:]) * y_rec
        h_ref[rs, :] += _dot(merged.astype(BF16), wo)

    @pl.when(kc == nkc - 1)
    def _():
        xn = _rms(h_ref[...], gffn_ref[...])
        xh = xn.astype(BF16)
        xhf = xh.astype(F32)
        xl = (xn - xhf).astype(BF16)
        half = xn.shape[1] // 2
        lo = pltpu.bitcast(xhf[:, :half], jnp.uint32) >> 16
        hi = pltpu.bitcast(xhf[:, half:], jnp.uint32) & jnp.uint32(0xFFFF0000)
        xq_ref[...] = lo | hi
        wrh = wrh_ref[...]
        logits = _dot(xh, wrh) + _dot(xh, wrl_ref[...]) + _dot(xl, wrh) + br_ref[...]
        lane = lax.broadcasted_iota(jnp.int32, (tm, LANES), 1)
        lane_f = lane.astype(F32)
        work = jnp.where(lane < N_EXPERTS, logits, NEG_BIG)
        vals, idxs, hots = [], [], []
        for _k in range(TOP_K):
            m = jnp.max(work, axis=-1, keepdims=True)
            idx = jnp.min(jnp.where(work == m, lane_f, float(LANES)), axis=-1, keepdims=True)
            hot = lane_f == idx
            work = jnp.where(hot, NEG_BIG, work)
            vals.append(m)
            idxs.append(idx)
            hots.append(hot)
        es = [jnp.exp(v - vals[0]) for v in vals]
        denom = es[0] + es[1] + es[2] + es[3]
        ws = [e / denom for e in es]
        hot_any = jnp.zeros((tm, LANES), F32)
        for hot in hots:
            hot_any = hot_any + hot.astype(F32)
        rr = lax.broadcasted_iota(jnp.int32, (tm, tm), 0)
        cc = lax.broadcasted_iota(jnp.int32, (tm, tm), 1)
        lower = jnp.where(rr > cc, 1.0, 0.0).astype(BF16)
        cum = _dot(lower, hot_any.astype(BF16)) + cnt_s[0:1, :]
        out = jnp.zeros((tm, LANES), F32)
        for k in range(TOP_K):
            rank = jnp.sum(jnp.where(hots[k], cum, 0.0), axis=-1, keepdims=True)
            out = jnp.where(lane == k, idxs[k], out)
            out = jnp.where(lane == TOP_K + k, ws[k], out)
            out = jnp.where(lane == 2 * TOP_K + k, rank, out)
        rt_ref[...] = out
        cnt_new = cnt_s[...] + jnp.sum(hot_any, axis=0, keepdims=True)
        cnt_s[...] = cnt_new
        cnt_ref[...] = cnt_new


def _mix2(x2, a, yc, hy, wg2, wa, wb, wo, gffn, wrh, wrl, br):
    n, d = x2.shape
    nkc, _, cw = wa.shape
    tm = TM_MIX
    grid = (n // tm, nkc)
    row = lambda i, k: (i, 0)
    const = lambda i, k: (0, 0)
    chunk = lambda i, k: (k, 0, 0)
    return pl.pallas_call(
        _mix2_kernel,
        grid=grid,
        in_specs=[
            pl.BlockSpec((tm, d), row),
            pl.BlockSpec((tm, d), row),
            pl.BlockSpec((tm, d), row),
            pl.BlockSpec((tm, d), row),
            pl.BlockSpec((1, d, 2 * cw), chunk),
            pl.BlockSpec((1, d, cw), chunk),
            pl.BlockSpec((1, d, cw), chunk),
            pl.BlockSpec((1, cw, d), chunk),
            pl.BlockSpec((1, d), const),
            pl.BlockSpec((d, LANES), const),
            pl.BlockSpec((d, LANES), const),
            pl.BlockSpec((1, LANES), const),
        ],
        out_specs=[
            pl.BlockSpec((tm, d), row),
            pl.BlockSpec((tm, d // 2), row),
            pl.BlockSpec((tm, LANES), row),
            pl.BlockSpec((SUBLANES, LANES), const),
        ],
        out_shape=[
            jax.ShapeDtypeStruct((n, d), F32),
            jax.ShapeDtypeStruct((n, d // 2), jnp.uint32),
            jax.ShapeDtypeStruct((n, LANES), F32),
            jax.ShapeDtypeStruct((SUBLANES, LANES), F32),
        ],
        scratch_shapes=[pltpu.VMEM((SUBLANES, LANES), F32)],
        compiler_params=pltpu.CompilerParams(
            dimension_semantics=("arbitrary", "arbitrary"), vmem_limit_bytes=VMEM_LIMIT),
        name="mix2",
    )(x2, a, yc, hy, wg2, wa, wb, wo, gffn, wrh, wrl, br)


def _moe_kernel(te_ref, ns_ref,
                tok_ref, xq_hbm, wg_ref, wu_ref, bg_ref, bu_ref, wd_ref, bd_ref,
                o_ref, gbuf, xb, sem):
    del te_ref
    i = pl.program_id(0)
    j = pl.program_id(1)
    nf = pl.num_programs(1)
    nt = pl.num_programs(0) - 1
    tr = o_ref.shape[0]
    sb = tr // 2
    fc = wg_ref.shape[2]
    half = xq_hbm.shape[1]
    per_step = tr // nf

    ns_next = jnp.where(i < nt, ns_ref[jnp.minimum(i, nt - 1)], 0)
    slot_n = i % 2
    issue_now = j * per_step < ns_next * sb
    ic = jnp.maximum(i - 1, 0)
    ns_cur = jnp.where(i > 0, ns_ref[ic], 0)
    slot_c = ic % 2

    def start_row(r):
        t = tok_ref[0, 0, r]
        pltpu.make_async_copy(xq_hbm.at[pl.ds(t, 1)], gbuf.at[slot_n, pl.ds(r, 1)],
                              sem.at[slot_n]).start()

    @pl.when(jnp.logical_and(ns_cur > 0, j == 0))
    def _():
        rows = ns_cur * sb
        pltpu.make_async_copy(xq_hbm.at[pl.ds(0, rows)], gbuf.at[slot_c, pl.ds(0, rows)],
                              sem.at[slot_c]).wait()

    for sub in range(2):
        lo_r, hi_r = sub * sb, (sub + 1) * sb

        @pl.when(jnp.logical_and(ns_cur > sub, j == 0))
        def _(lo_r=lo_r, hi_r=hi_r):
            w = gbuf[slot_c, lo_r:hi_r, :]
            xb[lo_r:hi_r, :half] = pltpu.bitcast(w << 16, F32).astype(BF16)
            xb[lo_r:hi_r, half:] = pltpu.bitcast(w & jnp.uint32(0xFFFF0000), F32).astype(BF16)
            o_ref[lo_r:hi_r, :] = jnp.broadcast_to(bd_ref[0], (sb, o_ref.shape[1]))

        @pl.when(jnp.logical_and(ns_cur <= sub, j == 0))
        def _(lo_r=lo_r, hi_r=hi_r):
            o_ref[lo_r:hi_r, :] = jnp.zeros((sb, o_ref.shape[1]), o_ref.dtype)

    def compute(nrows, with_issue):
        if with_issue:
            for r in range(per_step):
                start_row(j * per_step + r)
        wgu = jnp.concatenate([wg_ref[0], wu_ref[0]], axis=1).astype(BF16)
        wdn = wd_ref[0].astype(BF16)
        bg = bg_ref[0]
        bu = bu_ref[0]
        for c in range(nrows // MOE_CHAIN):
            rs = slice(c * MOE_CHAIN, (c + 1) * MOE_CHAIN)
            gu = _dot(xb[rs, :], wgu)
            gate = jnp.minimum(gu[:, :fc] + bg, SWIGLU_LIMIT)
            up = jnp.clip(gu[:, fc:] + bu, -SWIGLU_LIMIT, SWIGLU_LIMIT)
            glu = gate * jax.nn.sigmoid(SWIGLU_ALPHA * gate)
            act = ((up + 1.0) * glu).astype(BF16)
            o_ref[rs, :] += _dot(act, wdn)

    for nsub in (1, 2):
        for with_issue in (False, True):
            @pl.when(jnp.logical_and(ns_cur == nsub, issue_now == with_issue))
            def _(nsub=nsub, with_issue=with_issue):
                compute(nsub * sb, with_issue)

    @pl.when(jnp.logical_and(ns_cur == 0, issue_now))
    def _():
        def issue(r, carry):
            start_row(j * per_step + r)
            return carry

        lax.fori_loop(0, per_step, issue, 0, unroll=8)


def _moe(tile_e, tile_ns, slot_tok3, xq, wgu, bgu3, wd, bd3):
    n, half = xq.shape
    d = 2 * half
    nt, _, tr = slot_tok3.shape
    f = wd.shape[1]
    nf = f // FC

    def cur(i):
        return jnp.maximum(i - 1, 0)

    def jj(i, j, ns):
        return jnp.where(i == 0, 0, jnp.where(ns[cur(i)] > 0, j, nf - 1))

    grid_spec = pltpu.PrefetchScalarGridSpec(
        num_scalar_prefetch=2,
        grid=(nt + 1, nf),
        in_specs=[
            pl.BlockSpec((1, 1, tr), lambda i, j, te, ns: (jnp.minimum(i, nt - 1), 0, 0),
                         memory_space=pltpu.SMEM),
            pl.BlockSpec(memory_space=pl.ANY),
            pl.BlockSpec((1, d, FC), lambda i, j, te, ns: (te[cur(i)], 0, jj(i, j, ns))),
            pl.BlockSpec((1, d, FC), lambda i, j, te, ns: (te[cur(i)], 0, nf + jj(i, j, ns))),
            pl.BlockSpec((1, 1, FC), lambda i, j, te, ns: (te[cur(i)], 0, jj(i, j, ns))),
            pl.BlockSpec((1, 1, FC), lambda i, j, te, ns: (te[cur(i)], 0, nf + jj(i, j, ns))),
            pl.BlockSpec((1, FC, d), lambda i, j, te, ns: (te[cur(i)], jj(i, j, ns), 0)),
            pl.BlockSpec((1, 1, d), lambda i, j, te, ns: (te[cur(i)], 0, 0)),
        ],
        out_specs=pl.BlockSpec((tr, d), lambda i, j, te, ns: (cur(i), 0)),
        scratch_shapes=[
            pltpu.VMEM((2, tr, half), jnp.uint32),
            pltpu.VMEM((tr, d), BF16),
            pltpu.SemaphoreType.DMA((2,)),
        ],
    )
    return pl.pallas_call(
        _moe_kernel,
        grid_spec=grid_spec,
        out_shape=jax.ShapeDtypeStruct((nt * tr, d), F32),
        compiler_params=pltpu.CompilerParams(
            dimension_semantics=("arbitrary", "arbitrary"), vmem_limit_bytes=VMEM_LIMIT),
        name="moe",
    )(tile_e, tile_ns, slot_tok3, xq, wgu, wgu, bgu3, bgu3, wd, bd3)


def _combine_kernel(dst_ref, h_ref, ys_hbm, rt_ref, p_ref, gple_ref, wpg_ref, wpp_ref, gfin_ref,
                    o_ref, ybuf, sem):
    tm = h_ref.shape[0]

    def issue(r, carry):
        for k in range(TOP_K):
            s = dst_ref[0, 0, r * TOP_K + k]
            pltpu.make_async_copy(ys_hbm.at[pl.ds(s, 1)], ybuf.at[k, pl.ds(r, 1)], sem).start()
        return carry

    lax.fori_loop(0, tm, issue, 0, unroll=4)
    for k in range(TOP_K):
        pltpu.make_async_copy(ys_hbm.at[pl.ds(0, tm)], ybuf.at[k], sem).wait()

    rt = rt_ref[...]
    y = rt[:, TOP_K:TOP_K + 1] * ybuf[0]
    for k in range(1, TOP_K):
        y = y + rt[:, TOP_K + k:TOP_K + k + 1] * ybuf[k]
    h2 = h_ref[...] + y
    gate = jax.nn.sigmoid(_dot(_rms(h2, gple_ref[...]).astype(BF16), wpg_ref[...]))
    pp = _dot(p_ref[...].astype(BF16), wpp_ref[...])
    h3 = h2 + gate * pp
    o_ref[...] = _rms(h3, gfin_ref[...])


def _combine(dest3, h, ys, rt, p2, gple, wpg, wpp, gfin):
    n, d = h.shape
    tm = TM_OUT
    pd = p2.shape[1]
    row = lambda i: (i, 0)
    const = lambda i: (0, 0)
    return pl.pallas_call(
        _combine_kernel,
        grid=(n // tm,),
        in_specs=[
            pl.BlockSpec((1, 1, tm * TOP_K), lambda i: (i, 0, 0), memory_space=pltpu.SMEM),
            pl.BlockSpec((tm, d), row),
            pl.BlockSpec(memory_space=pl.ANY),
            pl.BlockSpec((tm, LANES), row),
            pl.BlockSpec((tm, pd), row),
            pl.BlockSpec((1, d), const),
            pl.BlockSpec((d, d), const),
            pl.BlockSpec((pd, d), const),
            pl.BlockSpec((1, d), const),
        ],
        out_specs=pl.BlockSpec((tm, d), row),
        out_shape=jax.ShapeDtypeStruct((n, d), F32),
        scratch_shapes=[
            pltpu.VMEM((TOP_K, tm, d), F32),
            pltpu.SemaphoreType.DMA(()),
        ],
        compiler_params=pltpu.CompilerParams(
            dimension_semantics=("arbitrary",), vmem_limit_bytes=VMEM_LIMIT),
        name="combine",
    )(dest3, h, ys, rt, p2, gple, wpg, wpp, gfin)


def _layer(h2, p2, norm_mix, w_in, conv_a_w, w_a_out, conv_b_w, conv_b_b, w_rg, b_rg, w_ig, b_ig,
           lru_lambda, w_b_out, w_o, norm_ffn, w_router, b_router, w_gu, b_gu, w_down, b_down,
           norm_ple, w_ple_gate, w_ple_proj, norm_out):
    n, d = h2.shape
    nch = LRU_HEADS
    cw = d // nch

    w5 = w_in[:, :5 * d].reshape(d, 5, nch, cw).transpose(2, 0, 1, 3).reshape(nch, d, 5 * cw).astype(BF16)
    wg2 = w_in[:, 5 * d:].reshape(d, 2, nch, cw).transpose(2, 0, 1, 3).reshape(nch, d, 2 * cw).astype(BF16)
    wa = w_a_out.reshape(d, nch, cw).transpose(1, 0, 2).astype(BF16)
    wb = w_b_out.reshape(d, nch, cw).transpose(1, 0, 2).astype(BF16)
    wo = w_o.reshape(nch, cw, d).astype(BF16)
    wgate = jnp.concatenate([w_rg, w_ig], axis=-1).astype(BF16)
    par = jnp.concatenate([conv_a_w, conv_b_w, conv_b_b[None], b_rg[None], b_ig[None],
                           lru_lambda[None], jnp.zeros((16 - 11, d), F32)], axis=0)
    wr_pad = jnp.pad(w_router, ((0, 0), (0, LANES - N_EXPERTS)))
    wrh = wr_pad.astype(BF16)
    wrl = (wr_pad - wrh.astype(F32)).astype(BF16)
    br = jnp.pad(b_router, (0, LANES - N_EXPERTS))[None]

    a, yc, hy = _mix1(h2, norm_mix[None], w5, par, wgate)
    h_mid, xq, rt, cnt = _mix2(h2, a, yc, hy, wg2, wa, wb, wo, norm_ffn[None], wrh, wrl, br)

    top_idx = rt[:, 0:TOP_K].astype(jnp.int32)
    rank = rt[:, 2 * TOP_K:3 * TOP_K].astype(jnp.int32)
    counts = cnt[0, :N_EXPERTS].astype(jnp.int32)
    sb = TR // 2
    nt = -(-(n * TOP_K) // TR) + N_EXPERTS
    n_sub = (counts + sb - 1) // sb
    n_tile = (n_sub + 1) // 2
    t_ends = jnp.cumsum(n_tile)
    t_starts = t_ends - n_tile
    n_tiles = t_ends[-1]
    dest = (t_starts * TR)[top_idx] + rank
    tok = jnp.broadcast_to(jnp.arange(n, dtype=jnp.int32)[:, None], (n, TOP_K))
    slot_tok = jnp.zeros((nt * TR,), jnp.int32).at[dest.reshape(-1)].set(tok.reshape(-1))
    ti = jnp.arange(nt, dtype=jnp.int32)
    e_of = jnp.minimum(jnp.sum((t_ends[None, :] <= ti[:, None]).astype(jnp.int32), axis=1), N_EXPERTS - 1)
    tile_ns = jnp.where(ti < n_tiles, jnp.clip(n_sub[e_of] - 2 * (ti - t_starts[e_of]), 0, 2), 0)
    tile_e = jnp.where(ti < n_tiles, e_of, e_of[n_tiles - 1])

    ys = _moe(tile_e.astype(jnp.int32), tile_ns.astype(jnp.int32), slot_tok.reshape(nt, 1, TR), xq,
              w_gu, b_gu[:, None, :], w_down, b_down[:, None, :])

    out = _combine(dest.reshape(n // TM_OUT, 1, TM_OUT * TOP_K), h_mid, ys, rt, p2, norm_ple[None],
                   w_ple_gate.astype(BF16), w_ple_proj.astype(BF16), norm_out[None])
    return out


def kernel(x, p, norm_mix, w_in, conv_a_w, w_a_out, conv_b_w, conv_b_b, w_rg, b_rg, w_ig, b_ig, lru_lambda, w_b_out, w_o, norm_ffn, w_router, b_router, w_gu, b_gu, w_down, b_down, norm_ple, w_ple_gate, w_ple_proj, norm_final):
    bsz, t, d = x.shape
    depth = p.shape[0]
    assert bsz == 1 and depth == 1, "kernel fuses the final norm into the single layer"
    out = _layer(x.reshape(t, d), p[0].reshape(t, -1), norm_mix[0], w_in[0], conv_a_w[0], w_a_out[0],
                 conv_b_w[0], conv_b_b[0], w_rg[0], b_rg[0], w_ig[0], b_ig[0], lru_lambda[0], w_b_out[0],
                 w_o[0], norm_ffn[0], w_router[0], b_router[0], w_gu[0], b_gu[0], w_down[0], b_down[0],
                 norm_ple[0], w_ple_gate[0], w_ple_proj[0], norm_final)
    return out.reshape(bsz, t, d)
```

```python
import functools

import jax
import jax.numpy as jnp
from jax import lax
from jax.experimental import pallas as pl
from jax.experimental.pallas import tpu as pltpu

F32 = jnp.float32
BF16 = jnp.bfloat16

NORM_EPS = 1e-6
LRU_HEADS = 8
LRU_C = 8.0
N_EXPERTS = 32
TOP_K = 4
SWIGLU_LIMIT = 7.0
SWIGLU_ALPHA = 1.702

LANES = 128
SUBLANES = 8
VMEM_LIMIT = 56 * 1024 * 1024

TM_MIX = 512
TM_OUT = 256
TR = 1024
FC = 256
MIX1_CPS = 2
MIX1_ROW_SPLIT = 2
MOE_CHAIN = 256
MIX2_CHAIN = 256
NEG_BIG = -3.0e38


def _rms(x, g):
    ms = jnp.mean(x * x, axis=-1, keepdims=True)
    return x * lax.rsqrt(ms + NORM_EPS) * g


def _dot(a, b):
    return jnp.dot(a, b, preferred_element_type=F32)


def _shift_rows(u, hist8, k):
    rolled = pltpu.roll(u, k, axis=0)
    hr = pltpu.roll(hist8, k, axis=0)
    row = lax.broadcasted_iota(jnp.int32, hist8.shape, 0)
    first = jnp.where(row < k, hr, rolled[:SUBLANES])
    return jnp.concatenate([first, rolled[SUBLANES:]], axis=0)


def _gelu_tanh(x):
    return 0.5 * x * (1.0 + jnp.tanh(0.7978845608028654 * (x + 0.044715 * (x * x * x))))


def _mix1_kernel(x_ref, gmix_ref, w5_ref, par_ref, wgate_ref,
                 a_ref, yc_ref, hy_ref,
                 hu_s, hx_s, hp_s, h_s):
    i = pl.program_id(0)
    c = pl.program_id(1)
    tm = yc_ref.shape[0]
    cps = w5_ref.shape[0]
    cw = yc_ref.shape[1] // cps

    @pl.when(jnp.logical_and(i == 0, c == 0))
    def _():
        hu_s[...] = jnp.zeros_like(hu_s)
        hx_s[...] = jnp.zeros_like(hx_s)
        hp_s[...] = jnp.zeros_like(hp_s)

    @pl.when(c == 0)
    def _():
        a_ref[...] = _rms(x_ref[...], gmix_ref[...]).astype(BF16)

    rt = tm // MIX1_ROW_SPLIT
    trow = lax.broadcasted_iota(jnp.int32, (rt, cw), 0)
    sub = trow % SUBLANES

    for q in range(cps):
        ch = c * cps + q
        cols = slice(q * cw, (q + 1) * cw)
        par = par_ref[:, cols]
        nlam = -par[10:11]
        softplus = jnp.maximum(nlam, 0.0) + jnp.log1p(jnp.exp(-jnp.abs(nlam)))
        hu = hu_s[ch]
        hx = hx_s[ch]
        carry = hp_s[ch][SUBLANES - 1:SUBLANES, :]
        for part in range(MIX1_ROW_SPLIT):
            rows = slice(part * rt, (part + 1) * rt)
            proj = _dot(a_ref[rows, :], w5_ref[q])
            b_c = proj[:, 0 * cw:1 * cw]
            c_c = proj[:, 1 * cw:2 * cw]
            v_c = proj[:, 2 * cw:3 * cw]
            y_r = proj[:, 3 * cw:4 * cw]
            x_r = proj[:, 4 * cw:5 * cw]

            u = c_c * v_c
            conv = (par[0:1] * _shift_rows(u, hu, 2) + par[1:2] * _shift_rows(u, hu, 1) + par[2:3] * u)
            yc_ref[rows, cols] = (b_c * conv).astype(BF16)
            hu = u[rt - SUBLANES:, :]

            xc = (par[3:4] * _shift_rows(x_r, hx, 3) + par[4:5] * _shift_rows(x_r, hx, 2)
                  + par[5:6] * _shift_rows(x_r, hx, 1) + par[6:7] * x_r) + par[7:8]
            hx = x_r[rt - SUBLANES:, :]
            gz = _dot(xc.astype(BF16), wgate_ref[q])
            r = jax.nn.sigmoid(gz[:, :cw] + par[8:9])
            ig = jax.nn.sigmoid(gz[:, cw:] + par[9:10])
            log_a = (-LRU_C) * r * softplus
            a = jnp.exp(log_a)
            mult = jnp.sqrt(1.0 - a * a)
            if part == 0:
                mult = jnp.where(trow + i * tm == 0, 1.0, mult)
            b = xc * ig * mult

            for s in (1, 2, 4):
                a_sh = pltpu.roll(a, s, axis=0)
                b_sh = pltpu.roll(b, s, axis=0)
                m = sub >= s
                b = jnp.where(m, a * b_sh + b, b)
                a = jnp.where(m, a * a_sh, a)
            for g in range(rt // SUBLANES):
                lo = g * SUBLANES
                hg = b[lo:lo + SUBLANES, :] + a[lo:lo + SUBLANES, :] * carry
                h_s[q, part * rt + lo:part * rt + lo + SUBLANES, :] = hg
                carry = hg[SUBLANES - 1:SUBLANES, :]
            hy_ref[rows, cols] = (h_s[q, rows, :] * _gelu_tanh(y_r)).astype(BF16)
        hu_s[ch] = hu
        hx_s[ch] = hx
        hp_s[ch] = h_s[q, tm - SUBLANES:, :]


def _mix1(x2, gmix, w5, par, wgate):
    n, d = x2.shape
    nch, _, w5c = w5.shape
    cw = w5c // 5
    tm = TM_MIX
    cps = MIX1_CPS
    grid = (n // tm, nch // cps)
    return pl.pallas_call(
        _mix1_kernel,
        grid=grid,
        in_specs=[
            pl.BlockSpec((tm, d), lambda i, c: (i, 0)),
            pl.BlockSpec((1, d), lambda i, c: (0, 0)),
            pl.BlockSpec((cps, d, w5c), lambda i, c: (c, 0, 0)),
            pl.BlockSpec((16, cps * cw), lambda i, c: (0, c)),
            pl.BlockSpec((cps, cw, 2 * cw), lambda i, c: (c, 0, 0)),
        ],
        out_specs=[
            pl.BlockSpec((tm, d), lambda i, c: (i, 0)),
            pl.BlockSpec((tm, cps * cw), lambda i, c: (i, c)),
            pl.BlockSpec((tm, cps * cw), lambda i, c: (i, c)),
        ],
        out_shape=[
            jax.ShapeDtypeStruct((n, d), BF16),
            jax.ShapeDtypeStruct((n, d), BF16),
            jax.ShapeDtypeStruct((n, d), BF16),
        ],
        scratch_shapes=[
            pltpu.VMEM((nch, SUBLANES, cw), F32),
            pltpu.VMEM((nch, SUBLANES, cw), F32),
            pltpu.VMEM((nch, SUBLANES, cw), F32),
            pltpu.VMEM((cps, tm, cw), F32),
        ],
        compiler_params=pltpu.CompilerParams(
            dimension_semantics=("arbitrary", "arbitrary"), vmem_limit_bytes=VMEM_LIMIT),
        name="mix1",
    )(x2, gmix, w5, par, wgate)


def _mix2_kernel(x_ref, a_ref, yc_ref, hy_ref, wg2_ref, wa_ref, wb_ref, wo_ref,
                 gffn_ref, wrh_ref, wrl_ref, br_ref,
                 h_ref, xq_ref, rt_ref, cnt_ref,
                 cnt_s):
    i = pl.program_id(0)
    kc = pl.program_id(1)
    nkc = pl.num_programs(1)
    tm = x_ref.shape[0]
    cw = wa_ref.shape[2]

    @pl.when(jnp.logical_and(i == 0, kc == 0))
    def _():
        cnt_s[...] = jnp.zeros_like(cnt_s)

    @pl.when(kc == 0)
    def _():
        h_ref[...] = x_ref[...]

    wg2 = wg2_ref[0]
    wa = wa_ref[0]
    wb = wb_ref[0]
    wo = wo_ref[0]
    for c in range(tm // MIX2_CHAIN):
        rs = slice(c * MIX2_CHAIN, (c + 1) * MIX2_CHAIN)
        g2 = _dot(a_ref[rs, :], wg2)
        y_conv = _dot(yc_ref[rs, :], wa)
        y_rec = _dot(hy_ref[rs, :], wb)
        merged = jax.nn.sigmoid(g2[:, :cw]) * y_conv + jax.nn.sigmoid(g2[:, cw:]) * y_rec
        h_ref[rs, :] += _dot(merged.astype(BF16), wo)

    @pl.when(kc == nkc - 1)
    def _():
        xn = _rms(h_ref[...], gffn_ref[...])
        xh = xn.astype(BF16)
        xhf = xh.astype(F32)
        xl = (xn - xhf).astype(BF16)
        half = xn.shape[1] // 2
        lo = pltpu.bitcast(xhf[:, :half], jnp.uint32) >> 16
        hi = pltpu.bitcast(xhf[:, half:], jnp.uint32) & jnp.uint32(0xFFFF0000)
        words = lo | hi
        tile_rows = half // LANES
        for q in range(tile_rows):
            xq_ref[pl.ds(q, tm, stride=tile_rows), :] = words[:, q * LANES:(q + 1) * LANES]
        wrh = wrh_ref[...]
        logits = _dot(xh, wrh) + _dot(xh, wrl_ref[...]) + _dot(xl, wrh) + br_ref[...]
        lane = lax.broadcasted_iota(jnp.int32, (tm, LANES), 1)
        lane_f = lane.astype(F32)
        work = jnp.where(lane < N_EXPERTS, logits, NEG_BIG)
        vals, idxs, hots = [], [], []
        for _k in range(TOP_K):
            m = jnp.max(work, axis=-1, keepdims=True)
            idx = jnp.min(jnp.where(work == m, lane_f, float(LANES)), axis=-1, keepdims=True)
            hot = lane_f == idx
            work = jnp.where(hot, NEG_BIG, work)
            vals.append(m)
            idxs.append(idx)
            hots.append(hot)
        es = [jnp.exp(v - vals[0]) for v in vals]
        denom = es[0] + es[1] + es[2] + es[3]
        ws = [e / denom for e in es]
        hot_any = jnp.zeros((tm, LANES), F32)
        for hot in hots:
            hot_any = hot_any + hot.astype(F32)
        rr = lax.broadcasted_iota(jnp.int32, (tm, tm), 0)
        cc = lax.broadcasted_iota(jnp.int32, (tm, tm), 1)
        lower = jnp.where(rr > cc, 1.0, 0.0).astype(BF16)
        cum = _dot(lower, hot_any.astype(BF16)) + cnt_s[0:1, :]
        out = jnp.zeros((tm, LANES), F32)
        for k in range(TOP_K):
            rank = jnp.sum(jnp.where(hots[k], cum, 0.0), axis=-1, keepdims=True)
            out = jnp.where(lane == k, idxs[k], out)
            out = jnp.where(lane == TOP_K + k, ws[k], out)
            out = jnp.where(lane == 2 * TOP_K + k, rank, out)
        rt_ref[...] = out
        cnt_new = cnt_s[...] + jnp.sum(hot_any, axis=0, keepdims=True)
        cnt_s[...] = cnt_new
        cnt_ref[...] = cnt_new


def _mix2(x2, a, yc, hy, wg2, wa, wb, wo, gffn, wrh, wrl, br):
    n, d = x2.shape
    nkc, _, cw = wa.shape
    tm = TM_MIX
    grid = (n // tm, nkc)
    row = lambda i, k: (i, 0)
    const = lambda i, k: (0, 0)
    chunk = lambda i, k: (k, 0, 0)
    return pl.pallas_call(
        _mix2_kernel,
        grid=grid,
        in_specs=[
            pl.BlockSpec((tm, d), row),
            pl.BlockSpec((tm, d), row),
            pl.BlockSpec((tm, d), row),
            pl.BlockSpec((tm, d), row),
            pl.BlockSpec((1, d, 2 * cw), chunk),
            pl.BlockSpec((1, d, cw), chunk),
            pl.BlockSpec((1, d, cw), chunk),
            pl.BlockSpec((1, cw, d), chunk),
            pl.BlockSpec((1, d), const),
            pl.BlockSpec((d, LANES), const),
            pl.BlockSpec((d, LANES), const),
            pl.BlockSpec((1, LANES), const),
        ],
        out_specs=[
            pl.BlockSpec((tm, d), row),
            pl.BlockSpec((tm * (d // 2 // LANES), LANES), row),
            pl.BlockSpec((tm, LANES), row),
            pl.BlockSpec((SUBLANES, LANES), const),
        ],
        out_shape=[
            jax.ShapeDtypeStruct((n, d), F32),
            jax.ShapeDtypeStruct((n * (d // 2 // LANES), LANES), jnp.uint32),
            jax.ShapeDtypeStruct((n, LANES), F32),
            jax.ShapeDtypeStruct((SUBLANES, LANES), F32),
        ],
        scratch_shapes=[pltpu.VMEM((SUBLANES, LANES), F32)],
        compiler_params=pltpu.CompilerParams(
            dimension_semantics=("arbitrary", "arbitrary"), vmem_limit_bytes=VMEM_LIMIT),
        name="mix2",
    )(x2, a, yc, hy, wg2, wa, wb, wo, gffn, wrh, wrl, br)


def _moe_kernel(te_ref, ns_ref,
                tok_ref, xq_hbm, wg_ref, wu_ref, bg_ref, bu_ref, wd_ref, bd_ref,
                o_ref, gbuf, xb, sem):
    del te_ref
    i = pl.program_id(0)
    j = pl.program_id(1)
    nf = pl.num_programs(1)
    nt = pl.num_programs(0) - 1
    tr = o_ref.shape[0]
    sb = tr // 2
    fc = wg_ref.shape[2]
    half = xb.shape[1] // 2
    tile_rows = half // LANES
    issue_steps = nf // 2
    per_step = tr // issue_steps

    ns_next = jnp.where(i < nt, ns_ref[jnp.minimum(i, nt - 1)], 0)
    slot_n = i % 2
    issue_now = j * per_step < ns_next * sb
    ic = jnp.maximum(i - 1, 0)
    ns_cur = jnp.where(i > 0, ns_ref[ic], 0)
    slot_c = ic % 2

    def start_row(r):
        src = pl.multiple_of(tok_ref[0, 0, r] * tile_rows, tile_rows)
        dst = pl.multiple_of(r * tile_rows, tile_rows)
        pltpu.make_async_copy(xq_hbm.at[pl.ds(src, tile_rows)], gbuf.at[slot_n, pl.ds(dst, tile_rows)],
                              sem.at[slot_n]).start()

    @pl.when(jnp.logical_and(ns_cur > 0, j == 0))
    def _():
        rows = ns_cur * (sb * tile_rows)
        pltpu.make_async_copy(xq_hbm.at[pl.ds(0, rows)], gbuf.at[slot_c, pl.ds(0, rows)],
                              sem.at[slot_c]).wait()

    for sub in range(2):
        lo_r, hi_r = sub * sb, (sub + 1) * sb

        @pl.when(jnp.logical_and(ns_cur > sub, j == 0))
        def _(lo_r=lo_r, hi_r=hi_r):
            for q in range(tile_rows):
                w = gbuf[slot_c, pl.ds(lo_r * tile_rows + q, sb, stride=tile_rows), :]
                cl = slice(q * LANES, (q + 1) * LANES)
                ch = slice(half + q * LANES, half + (q + 1) * LANES)
                xb[lo_r:hi_r, cl] = pltpu.bitcast(w << 16, F32).astype(BF16)
                xb[lo_r:hi_r, ch] = pltpu.bitcast(w & jnp.uint32(0xFFFF0000), F32).astype(BF16)
            o_ref[lo_r:hi_r, :] = jnp.broadcast_to(bd_ref[0], (sb, o_ref.shape[1]))

        @pl.when(jnp.logical_and(ns_cur <= sub, j == 0))
        def _(lo_r=lo_r, hi_r=hi_r):
            o_ref[lo_r:hi_r, :] = jnp.zeros((sb, o_ref.shape[1]), o_ref.dtype)

    def compute(nrows, with_issue):
        if with_issue:
            for r in range(per_step):
                start_row(j * per_step + r)
        wgu = jnp.concatenate([wg_ref[0], wu_ref[0]], axis=1).astype(BF16)
        wdn = wd_ref[0].astype(BF16)
        bg = bg_ref[0]
        bu = bu_ref[0]
        for c in range(nrows // MOE_CHAIN):
            rs = slice(c * MOE_CHAIN, (c + 1) * MOE_CHAIN)
            gu = _dot(xb[rs, :], wgu)
            gate = jnp.minimum(gu[:, :fc] + bg, SWIGLU_LIMIT)
            up = jnp.clip(gu[:, fc:] + bu, -SWIGLU_LIMIT, SWIGLU_LIMIT)
            glu = gate * jax.nn.sigmoid(SWIGLU_ALPHA * gate)
            act = ((up + 1.0) * glu).astype(BF16)
            o_ref[rs, :] += _dot(act, wdn)

    for nsub in (1, 2):
        for with_issue in (False, True):
            @pl.when(jnp.logical_and(ns_cur == nsub, issue_now == with_issue))
            def _(nsub=nsub, with_issue=with_issue):
                compute(nsub * sb, with_issue)

    @pl.when(jnp.logical_and(ns_cur == 0, issue_now))
    def _():
        def issue(r, carry):
            start_row(j * per_step + r)
            return carry

        lax.fori_loop(0, per_step, issue, 0, unroll=8)


def _moe(tile_e, tile_ns, slot_tok3, xq, wgu, bgu3, wd, bd3):
    d = wgu.shape[1]
    half = d // 2
    nt, _, tr = slot_tok3.shape
    f = wd.shape[1]
    nf = f // FC

    def cur(i):
        return jnp.maximum(i - 1, 0)

    def jj(i, j, ns):
        return jnp.where(i == 0, 0, jnp.where(ns[cur(i)] > 0, j, nf - 1))

    grid_spec = pltpu.PrefetchScalarGridSpec(
        num_scalar_prefetch=2,
        grid=(nt + 1, nf),
        in_specs=[
            pl.BlockSpec((1, 1, tr), lambda i, j, te, ns: (jnp.minimum(i, nt - 1), 0, 0),
                         memory_space=pltpu.SMEM),
            pl.BlockSpec(memory_space=pl.ANY),
            pl.BlockSpec((1, d, FC), lambda i, j, te, ns: (te[cur(i)], 0, jj(i, j, ns))),
            pl.BlockSpec((1, d, FC), lambda i, j, te, ns: (te[cur(i)], 0, nf + jj(i, j, ns))),
            pl.BlockSpec((1, 1, FC), lambda i, j, te, ns: (te[cur(i)], 0, jj(i, j, ns))),
            pl.BlockSpec((1, 1, FC), lambda i, j, te, ns: (te[cur(i)], 0, nf + jj(i, j, ns))),
            pl.BlockSpec((1, FC, d), lambda i, j, te, ns: (te[cur(i)], jj(i, j, ns), 0)),
            pl.BlockSpec((1, 1, d), lambda i, j, te, ns: (te[cur(i)], 0, 0)),
        ],
        out_specs=pl.BlockSpec((tr, d), lambda i, j, te, ns: (cur(i), 0)),
        scratch_shapes=[
            pltpu.VMEM((2, tr * (half // LANES), LANES), jnp.uint32),
            pltpu.VMEM((tr, d), BF16),
            pltpu.SemaphoreType.DMA((2,)),
        ],
    )
    return pl.pallas_call(
        _moe_kernel,
        grid_spec=grid_spec,
        out_shape=jax.ShapeDtypeStruct((nt * tr, d), F32),
        compiler_params=pltpu.CompilerParams(
            dimension_semantics=("arbitrary", "arbitrary"), vmem_limit_bytes=VMEM_LIMIT),
        name="moe",
    )(tile_e, tile_ns, slot_tok3, xq, wgu, wgu, bgu3, bgu3, wd, bd3)


def _combine_kernel(dst_ref, h_ref, ys_hbm, rt_ref, p_ref, gple_ref, wpg_ref, wpp_ref, gfin_ref,
                    o_ref, ybuf, sem):
    tm = h_ref.shape[0]

    def issue(r, carry):
        for k in range(TOP_K):
            s = dst_ref[0, 0, r * TOP_K + k]
            pltpu.make_async_copy(ys_hbm.at[pl.ds(s, 1)], ybuf.at[k, pl.ds(r, 1)], sem).start(priority=k % 2)
        return carry

    lax.fori_loop(0, tm, issue, 0, unroll=4)
    for k in range(TOP_K):
        pltpu.make_async_copy(ys_hbm.at[pl.ds(0, tm)], ybuf.at[k], sem).wait()

    rt = rt_ref[...]
    y = rt[:, TOP_K:TOP_K + 1] * ybuf[0]
    for k in range(1, TOP_K):
        y = y + rt[:, TOP_K + k:TOP_K + k + 1] * ybuf[k]
    h2 = h_ref[...] + y
    gate = jax.nn.sigmoid(_dot(_rms(h2, gple_ref[...]).astype(BF16), wpg_ref[...]))
    pp = _dot(p_ref[...].astype(BF16), wpp_ref[...])
    h3 = h2 + gate * pp
    o_ref[...] = _rms(h3, gfin_ref[...])


def _combine(dest3, h, ys, rt, p2, gple, wpg, wpp, gfin):
    n, d = h.shape
    tm = TM_OUT
    pd = p2.shape[1]
    row = lambda i: (i, 0)
    const = lambda i: (0, 0)
    return pl.pallas_call(
        _combine_kernel,
        grid=(n // tm,),
        in_specs=[
            pl.BlockSpec((1, 1, tm * TOP_K), lambda i: (i, 0, 0), memory_space=pltpu.SMEM),
            pl.BlockSpec((tm, d), row),
            pl.BlockSpec(memory_space=pl.ANY),
            pl.BlockSpec((tm, LANES), row),
            pl.BlockSpec((tm, pd), row),
            pl.BlockSpec((1, d), const),
            pl.BlockSpec((d, d), const),
            pl.BlockSpec((pd, d), const),
            pl.BlockSpec((1, d), const),
        ],
        out_specs=pl.BlockSpec((tm, d), row),
        out_shape=jax.ShapeDtypeStruct((n, d), F32),
        scratch_shapes=[
            pltpu.VMEM((TOP_K, tm, d), F32),
            pltpu.SemaphoreType.DMA(()),
        ],
        compiler_params=pltpu.CompilerParams(
            dimension_semantics=("arbitrary",), vmem_limit_bytes=VMEM_LIMIT),
        name="combine",
    )(dest3, h, ys, rt, p2, gple, wpg, wpp, gfin)


def _layer(h2, p2, norm_mix, w_in, conv_a_w, w_a_out, conv_b_w, conv_b_b, w_rg, b_rg, w_ig, b_ig,
           lru_lambda, w_b_out, w_o, norm_ffn, w_router, b_router, w_gu, b_gu, w_down, b_down,
           norm_ple, w_ple_gate, w_ple_proj, norm_out):
    n, d = h2.shape
    nch = LRU_HEADS
    cw = d // nch

    w5 = w_in[:, :5 * d].reshape(d, 5, nch, cw).transpose(2, 0, 1, 3).reshape(nch, d, 5 * cw).astype(BF16)
    wg2 = w_in[:, 5 * d:].reshape(d, 2, nch, cw).transpose(2, 0, 1, 3).reshape(nch, d, 2 * cw).astype(BF16)
    wa = w_a_out.reshape(d, nch, cw).transpose(1, 0, 2).astype(BF16)
    wb = w_b_out.reshape(d, nch, cw).transpose(1, 0, 2).astype(BF16)
    wo = w_o.reshape(nch, cw, d).astype(BF16)
    wgate = jnp.concatenate([w_rg, w_ig], axis=-1).astype(BF16)
    par = jnp.concatenate([conv_a_w, conv_b_w, conv_b_b[None], b_rg[None], b_ig[None],
                           lru_lambda[None], jnp.zeros((16 - 11, d), F32)], axis=0)
    wr_pad = jnp.pad(w_router, ((0, 0), (0, LANES - N_EXPERTS)))
    wrh = wr_pad.astype(BF16)
    wrl = (wr_pad - wrh.astype(F32)).astype(BF16)
    br = jnp.pad(b_router, (0, LANES - N_EXPERTS))[None]

    a, yc, hy = _mix1(h2, norm_mix[None], w5, par, wgate)
    h_mid, xq, rt, cnt = _mix2(h2, a, yc, hy, wg2, wa, wb, wo, norm_ffn[None], wrh, wrl, br)

    top_idx = rt[:, 0:TOP_K].astype(jnp.int32)
    rank = rt[:, 2 * TOP_K:3 * TOP_K].astype(jnp.int32)
    counts = cnt[0, :N_EXPERTS].astype(jnp.int32)
    sb = TR // 2
    nt = -(-(n * TOP_K) // TR) + N_EXPERTS
    n_sub = (counts + sb - 1) // sb
    n_tile = (n_sub + 1) // 2
    t_ends = jnp.cumsum(n_tile)
    t_starts = t_ends - n_tile
    n_tiles = t_ends[-1]
    dest = (t_starts * TR)[top_idx] + rank
    tok = jnp.broadcast_to(jnp.arange(n, dtype=jnp.int32)[:, None], (n, TOP_K))
    slot_tok = jnp.zeros((nt * TR,), jnp.int32).at[dest.reshape(-1)].set(tok.reshape(-1))
    ti = jnp.arange(nt, dtype=jnp.int32)
    e_of = jnp.minimum(jnp.sum((t_ends[None, :] <= ti[:, None]).astype(jnp.int32), axis=1), N_EXPERTS - 1)
    tile_ns = jnp.where(ti < n_tiles, jnp.clip(n_sub[e_of] - 2 * (ti - t_starts[e_of]), 0, 2), 0)
    tile_e = jnp.where(ti < n_tiles, e_of, e_of[n_tiles - 1])

    ys = _moe(tile_e.astype(jnp.int32), tile_ns.astype(jnp.int32), slot_tok.reshape(nt, 1, TR), xq,
              w_gu, b_gu[:, None, :], w_down, b_down[:, None, :])

    out = _combine(dest.reshape(n // TM_OUT, 1, TM_OUT * TOP_K), h_mid, ys, rt, p2, norm_ple[None],
                   w_ple_gate.astype(BF16), w_ple_proj.astype(BF16), norm_out[None])
    return out


def kernel(x, p, norm_mix, w_in, conv_a_w, w_a_out, conv_b_w, conv_b_b, w_rg, b_rg, w_ig, b_ig, lru_lambda, w_b_out, w_o, norm_ffn, w_router, b_router, w_gu, b_gu, w_down, b_down, norm_ple, w_ple_gate, w_ple_proj, norm_final):
    bsz, t, d = x.shape
    depth = p.shape[0]
    assert bsz == 1 and depth == 1, "kernel fuses the final norm into the single layer"
    out = _layer(x.reshape(t, d), p[0].reshape(t, -1), norm_mix[0], w_in[0], conv_a_w[0], w_a_out[0],
                 conv_b_w[0], conv_b_b[0], w_rg[0], b_rg[0], w_ig[0], b_ig[0], lru_lambda[0], w_b_out[0],
                 w_o[0], norm_ffn[0], w_router[0], b_router[0], w_gu[0], b_gu[0], w_down[0], b_down[0],
                 norm_ple[0], w_ple_gate[0], w_ple_proj[0], norm_final)
    return out.reshape(bsz, t, d)
```

```python
import functools

import jax
import jax.numpy as jnp
from jax import lax
from jax.experimental import pallas as pl
from jax.experimental.pallas import tpu as pltpu

F32 = jnp.float32
BF16 = jnp.bfloat16

NORM_EPS = 1e-6
LRU_HEADS = 8
LRU_C = 8.0
N_EXPERTS = 32
TOP_K = 4
SWIGLU_LIMIT = 7.0
SWIGLU_ALPHA = 1.702

LANES = 128
SUBLANES = 8
VMEM_LIMIT = 56 * 1024 * 1024

TM_MIX = 512
TM_OUT = 256
TR = 1024
FC = 256
MIX1_CPS = 2
MIX1_ROW_SPLIT = 2
MOE_CHAIN = 256
MIX2_CHAIN = 256
NEG_BIG = -3.0e38


def _rms(x, g):
    ms = jnp.mean(x * x, axis=-1, keepdims=True)
    return x * lax.rsqrt(ms + NORM_EPS) * g


def _dot(a, b):
    return jnp.dot(a, b, preferred_element_type=F32)


def _shift_rows(u, hist8, k):
    rolled = pltpu.roll(u, k, axis=0)
    hr = pltpu.roll(hist8, k, axis=0)
    row = lax.broadcasted_iota(jnp.int32, hist8.shape, 0)
    first = jnp.where(row < k, hr, rolled[:SUBLANES])
    return jnp.concatenate([first, rolled[SUBLANES:]], axis=0)


def _gelu_tanh(x):
    return 0.5 * x * (1.0 + jnp.tanh(0.7978845608028654 * (x + 0.044715 * (x * x * x))))


def _mix1_kernel(x_ref, gmix_ref, w5_ref, par_ref, wgate_ref,
                 a_ref, yc_ref, hy_ref,
                 hu_s, hx_s, hp_s, h_s):
    i = pl.program_id(0)
    c = pl.program_id(1)
    tm = yc_ref.shape[0]
    cps = w5_ref.shape[0]
    cw = yc_ref.shape[1] // cps

    @pl.when(jnp.logical_and(i == 0, c == 0))
    def _():
        hu_s[...] = jnp.zeros_like(hu_s)
        hx_s[...] = jnp.zeros_like(hx_s)
        hp_s[...] = jnp.zeros_like(hp_s)

    @pl.when(c == 0)
    def _():
        a_ref[...] = _rms(x_ref[...], gmix_ref[...]).astype(BF16)

    rt = tm // MIX1_ROW_SPLIT
    trow = lax.broadcasted_iota(jnp.int32, (rt, cw), 0)
    sub = trow % SUBLANES

    for q in range(cps):
        ch = c * cps + q
        cols = slice(q * cw, (q + 1) * cw)
        par = par_ref[:, cols]
        nlam = -par[10:11]
        softplus = jnp.maximum(nlam, 0.0) + jnp.log1p(jnp.exp(-jnp.abs(nlam)))
        hu = hu_s[ch]
        hx = hx_s[ch]
        carry = hp_s[ch][SUBLANES - 1:SUBLANES, :]
        for part in range(MIX1_ROW_SPLIT):
            rows = slice(part * rt, (part + 1) * rt)
            proj = _dot(a_ref[rows, :], w5_ref[q])
            b_c = proj[:, 0 * cw:1 * cw]
            c_c = proj[:, 1 * cw:2 * cw]
            v_c = proj[:, 2 * cw:3 * cw]
            y_r = proj[:, 3 * cw:4 * cw]
            x_r = proj[:, 4 * cw:5 * cw]

            u = c_c * v_c
            conv = (par[0:1] * _shift_rows(u, hu, 2) + par[1:2] * _shift_rows(u, hu, 1) + par[2:3] * u)
            yc_ref[rows, cols] = (b_c * conv).astype(BF16)
            hu = u[rt - SUBLANES:, :]

            xc = (par[3:4] * _shift_rows(x_r, hx, 3) + par[4:5] * _shift_rows(x_r, hx, 2)
                  + par[5:6] * _shift_rows(x_r, hx, 1) + par[6:7] * x_r) + par[7:8]
            hx = x_r[rt - SUBLANES:, :]
            gz = _dot(xc.astype(BF16), wgate_ref[q])
            r = jax.nn.sigmoid(gz[:, :cw] + par[8:9])
            ig = jax.nn.sigmoid(gz[:, cw:] + par[9:10])
            log_a = (-LRU_C) * r * softplus
            a = jnp.exp(log_a)
            mult = jnp.sqrt(1.0 - a * a)
            if part == 0:
                mult = jnp.where(trow + i * tm == 0, 1.0, mult)
            b = xc * ig * mult

            for s in (1, 2, 4):
                a_sh = pltpu.roll(a, s, axis=0)
                b_sh = pltpu.roll(b, s, axis=0)
                m = sub >= s
                b = jnp.where(m, a * b_sh + b, b)
                a = jnp.where(m, a * a_sh, a)
            for g in range(rt // SUBLANES):
                lo = g * SUBLANES
                hg = b[lo:lo + SUBLANES, :] + a[lo:lo + SUBLANES, :] * carry
                h_s[q, part * rt + lo:part * rt + lo + SUBLANES, :] = hg
                carry = hg[SUBLANES - 1:SUBLANES, :]
            hy_ref[rows, cols] = (h_s[q, rows, :] * _gelu_tanh(y_r)).astype(BF16)
        hu_s[ch] = hu
        hx_s[ch] = hx
        hp_s[ch] = h_s[q, tm - SUBLANES:, :]


def _mix1(x2, gmix, w5, par, wgate):
    n, d = x2.shape
    nch, _, w5c = w5.shape
    cw = w5c // 5
    tm = TM_MIX
    cps = MIX1_CPS
    grid = (n // tm, nch // cps)
    return pl.pallas_call(
        _mix1_kernel,
        grid=grid,
        in_specs=[
            pl.BlockSpec((tm, d), lambda i, c: (i, 0)),
            pl.BlockSpec((1, d), lambda i, c: (0, 0)),
            pl.BlockSpec((cps, d, w5c), lambda i, c: (c, 0, 0)),
            pl.BlockSpec((16, cps * cw), lambda i, c: (0, c)),
            pl.BlockSpec((cps, cw, 2 * cw), lambda i, c: (c, 0, 0)),
        ],
        out_specs=[
            pl.BlockSpec((tm, d), lambda i, c: (i, 0)),
            pl.BlockSpec((tm, cps * cw), lambda i, c: (i, c)),
            pl.BlockSpec((tm, cps * cw), lambda i, c: (i, c)),
        ],
        out_shape=[
            jax.ShapeDtypeStruct((n, d), BF16),
            jax.ShapeDtypeStruct((n, d), BF16),
            jax.ShapeDtypeStruct((n, d), BF16),
        ],
        scratch_shapes=[
            pltpu.VMEM((nch, SUBLANES, cw), F32),
            pltpu.VMEM((nch, SUBLANES, cw), F32),
            pltpu.VMEM((nch, SUBLANES, cw), F32),
            pltpu.VMEM((cps, tm, cw), F32),
        ],
        compiler_params=pltpu.CompilerParams(
            dimension_semantics=("arbitrary", "arbitrary"), vmem_limit_bytes=VMEM_LIMIT),
        name="mix1",
    )(x2, gmix, w5, par, wgate)


def _mix2_kernel(x_ref, a_ref, yc_ref, hy_ref, wg2_ref, wa_ref, wb_ref, wo_ref,
                 gffn_ref, wrh_ref, wrl_ref, br_ref,
                 h_ref, xq_ref, rt_ref, cnt_ref,
                 cnt_s):
    i = pl.program_id(0)
    kc = pl.program_id(1)
    nkc = pl.num_programs(1)
    tm = x_ref.shape[0]
    cw = wa_ref.shape[2]

    @pl.when(jnp.logical_and(i == 0, kc == 0))
    def _():
        cnt_s[...] = jnp.zeros_like(cnt_s)

    @pl.when(kc == 0)
    def _():
        h_ref[...] = x_ref[...]

    wg2 = wg2_ref[0]
    wa = wa_ref[0]
    wb = wb_ref[0]
    wo = wo_ref[0]
    for c in range(tm // MIX2_CHAIN):
        rs = slice(c * MIX2_CHAIN, (c + 1) * MIX2_CHAIN)
        g2 = _dot(a_ref[rs, :], wg2)
        y_conv = _dot(yc_ref[rs, :], wa)
        y_rec = _dot(hy_ref[rs, :], wb)
        merged = jax.nn.sigmoid(g2[:, :cw]) * y_conv + jax.nn.sigmoid(g2[:, cw:]) * y_rec
        h_ref[rs, :] += _dot(merged.astype(BF16), wo)

    @pl.when(kc == nkc - 1)
    def _():
        xn = _rms(h_ref[...], gffn_ref[...])
        xh = xn.astype(BF16)
        xhf = xh.astype(F32)
        xl = (xn - xhf).astype(BF16)
        half = xn.shape[1] // 2
        lo = pltpu.bitcast(xhf[:, :half], jnp.uint32) >> 16
        hi = pltpu.bitcast(xhf[:, half:], jnp.uint32) & jnp.uint32(0xFFFF0000)
        words = lo | hi
        tile_rows = half // LANES
        for q in range(tile_rows):
            xq_ref[pl.ds(q, tm, stride=tile_rows), :] = words[:, q * LANES:(q + 1) * LANES]
        wrh = wrh_ref[...]
        logits = _dot(xh, wrh) + _dot(xh, wrl_ref[...]) + _dot(xl, wrh) + br_ref[...]
        lane = lax.broadcasted_iota(jnp.int32, (tm, LANES), 1)
        lane_f = lane.astype(F32)
        work = jnp.where(lane < N_EXPERTS, logits, NEG_BIG)
        vals, idxs, hots = [], [], []
        for _k in range(TOP_K):
            m = jnp.max(work, axis=-1, keepdims=True)
            idx = jnp.min(jnp.where(work == m, lane_f, float(LANES)), axis=-1, keepdims=True)
            hot = lane_f == idx
            work = jnp.where(hot, NEG_BIG, work)
            vals.append(m)
            idxs.append(idx)
            hots.append(hot)
        es = [jnp.exp(v - vals[0]) for v in vals]
        denom = es[0] + es[1] + es[2] + es[3]
        ws = [e / denom for e in es]
        hot_any = jnp.zeros((tm, LANES), F32)
        for hot in hots:
            hot_any = hot_any + hot.astype(F32)
        rr = lax.broadcasted_iota(jnp.int32, (tm, tm), 0)
        cc = lax.broadcasted_iota(jnp.int32, (tm, tm), 1)
        lower = jnp.where(rr > cc, 1.0, 0.0).astype(BF16)
        cum = _dot(lower, hot_any.astype(BF16)) + cnt_s[0:1, :]
        out = jnp.zeros((tm, LANES), F32)
        for k in range(TOP_K):
            rank = jnp.sum(jnp.where(hots[k], cum, 0.0), axis=-1, keepdims=True)
            out = jnp.where(lane == k, idxs[k], out)
            out = jnp.where(lane == TOP_K + k, ws[k], out)
            out = jnp.where(lane == 2 * TOP_K + k, rank, out)
        rt_ref[...] = out
        cnt_new = cnt_s[...] + jnp.sum(hot_any, axis=0, keepdims=True)
        cnt_s[...] = cnt_new
        cnt_ref[...] = cnt_new


def _mix2(x2, a, yc, hy, wg2, wa, wb, wo, gffn, wrh, wrl, br):
    n, d = x2.shape
    nkc, _, cw = wa.shape
    tm = TM_MIX
    grid = (n // tm, nkc)
    row = lambda i, k: (i, 0)
    const = lambda i, k: (0, 0)
    chunk = lambda i, k: (k, 0, 0)
    return pl.pallas_call(
        _mix2_kernel,
        grid=grid,
        in_specs=[
            pl.BlockSpec((tm, d), row),
            pl.BlockSpec((tm, d), row),
            pl.BlockSpec((tm, d), row),
            pl.BlockSpec((tm, d), row),
            pl.BlockSpec((1, d, 2 * cw), chunk),
            pl.BlockSpec((1, d, cw), chunk),
            pl.BlockSpec((1, d, cw), chunk),
            pl.BlockSpec((1, cw, d), chunk),
            pl.BlockSpec((1, d), const),
            pl.BlockSpec((d, LANES), const),
            pl.BlockSpec((d, LANES), const),
            pl.BlockSpec((1, LANES), const),
        ],
        out_specs=[
            pl.BlockSpec((tm, d), row),
            pl.BlockSpec((tm * (d // 2 // LANES), LANES), row),
            pl.BlockSpec((tm, LANES), row),
            pl.BlockSpec((SUBLANES, LANES), const),
        ],
        out_shape=[
            jax.ShapeDtypeStruct((n, d), F32),
            jax.ShapeDtypeStruct((n * (d // 2 // LANES), LANES), jnp.uint32),
            jax.ShapeDtypeStruct((n, LANES), F32),
            jax.ShapeDtypeStruct((SUBLANES, LANES), F32),
        ],
        scratch_shapes=[pltpu.VMEM((SUBLANES, LANES), F32)],
        compiler_params=pltpu.CompilerParams(
            dimension_semantics=("arbitrary", "arbitrary"), vmem_limit_bytes=VMEM_LIMIT),
        name="mix2",
    )(x2, a, yc, hy, wg2, wa, wb, wo, gffn, wrh, wrl, br)


def _moe_kernel(te_ref, ns_ref,
                tok_ref, xq_hbm, wgu_hbm, wd_hbm, bgu_ref, bd_ref,
                o_ref, gbuf, xb, wg_s, wu_s, wd_s, gsem, wsem):
    i = pl.program_id(0)
    nt = pl.num_programs(0) - 1
    tr, d = o_ref.shape
    sb = tr // 2
    fc = wg_s.shape[2]
    f = wd_hbm.shape[1]
    nf = f // fc
    half = d // 2
    tile_rows = half // LANES
    per_chunk = tr // (nf // 2)

    i_n = jnp.minimum(i, nt - 1)
    ns_next = jnp.where(i < nt, ns_ref[i_n], 0)
    slot_n = i % 2
    issue_iters = ns_next * (sb // per_chunk)
    ic = jnp.maximum(i - 1, 0)
    ns_cur = jnp.where(i > 0, ns_ref[ic], 0)
    slot_c = ic % 2
    e_cur = te_ref[ic]

    def start_row(r):
        src = pl.multiple_of(tok_ref[0, 0, r] * tile_rows, tile_rows)
        dst = pl.multiple_of(r * tile_rows, tile_rows)
        pltpu.make_async_copy(xq_hbm.at[pl.ds(src, tile_rows)], gbuf.at[slot_n, pl.ds(dst, tile_rows)],
                              gsem.at[slot_n]).start()

    def weight_copies(e, j, slot):
        col = pl.multiple_of(j * fc, fc)
        ucol = pl.multiple_of(f + j * fc, fc)
        return (
            pltpu.make_async_copy(wgu_hbm.at[e, :, pl.ds(col, fc)], wg_s.at[slot], wsem.at[slot, 0]),
            pltpu.make_async_copy(wgu_hbm.at[e, :, pl.ds(ucol, fc)], wu_s.at[slot], wsem.at[slot, 1]),
            pltpu.make_async_copy(wd_hbm.at[e, pl.ds(col, fc), :], wd_s.at[slot], wsem.at[slot, 2]),
        )

    @pl.when(jnp.logical_and(i == 0, ns_next > 0))
    def _():
        for cp in weight_copies(te_ref[0], 0, 0):
            cp.start()

    @pl.when(ns_cur > 0)
    def _():
        rows = ns_cur * (sb * tile_rows)
        pltpu.make_async_copy(xq_hbm.at[pl.ds(0, rows)], gbuf.at[slot_c, pl.ds(0, rows)],
                              gsem.at[slot_c]).wait()

    for sub in range(2):
        lo_r, hi_r = sub * sb, (sub + 1) * sb

        @pl.when(ns_cur > sub)
        def _(lo_r=lo_r, hi_r=hi_r):
            for q in range(tile_rows):
                w = gbuf[slot_c, pl.ds(lo_r * tile_rows + q, sb, stride=tile_rows), :]
                cl = slice(q * LANES, (q + 1) * LANES)
                ch = slice(half + q * LANES, half + (q + 1) * LANES)
                xb[lo_r:hi_r, cl] = pltpu.bitcast(w << 16, F32).astype(BF16)
                xb[lo_r:hi_r, ch] = pltpu.bitcast(w & jnp.uint32(0xFFFF0000), F32).astype(BF16)
            o_ref[lo_r:hi_r, :] = jnp.broadcast_to(bd_ref[0], (sb, d))

        @pl.when(ns_cur <= sub)
        def _(lo_r=lo_r, hi_r=hi_r):
            o_ref[lo_r:hi_r, :] = jnp.zeros((sb, d), o_ref.dtype)

    def chunk_body(j, nrows, with_issue):
        slot = j % 2
        for cp in weight_copies(e_cur, j, slot):
            cp.wait()
        last = j == nf - 1
        j_n = jnp.where(last, 0, j + 1)
        e_n = jnp.where(last, te_ref[i_n], e_cur)

        @pl.when(jnp.logical_or(jnp.logical_not(last), ns_next > 0))
        def _():
            for cp in weight_copies(e_n, j_n, 1 - slot):
                cp.start()

        if with_issue:
            for r in range(per_chunk):
                start_row(j * per_chunk + r)
        wgu = jnp.concatenate([wg_s[slot], wu_s[slot]], axis=1).astype(BF16)
        wdn = wd_s[slot].astype(BF16)
        bg = bgu_ref[0, pl.ds(j, 1), :]
        bu = bgu_ref[0, pl.ds(nf + j, 1), :]
        for c in range(nrows // MOE_CHAIN):
            rs = slice(c * MOE_CHAIN, (c + 1) * MOE_CHAIN)
            gu = _dot(xb[rs, :], wgu)
            gate = jnp.minimum(gu[:, :fc] + bg, SWIGLU_LIMIT)
            up = jnp.clip(gu[:, fc:] + bu, -SWIGLU_LIMIT, SWIGLU_LIMIT)
            glu = gate * jax.nn.sigmoid(SWIGLU_ALPHA * gate)
            act = ((up + 1.0) * glu).astype(BF16)
            o_ref[rs, :] += _dot(act, wdn)

    for nsub in (1, 2):
        @pl.when(ns_cur == nsub)
        def _(nsub=nsub):
            def with_issue(j, carry):
                chunk_body(j, nsub * sb, True)
                return carry

            def without_issue(j, carry):
                chunk_body(j, nsub * sb, False)
                return carry

            lax.fori_loop(0, issue_iters, with_issue, 0)
            lax.fori_loop(issue_iters, nf, without_issue, 0)

    @pl.when(jnp.logical_and(ns_cur == 0, ns_next > 0))
    def _():
        def issue(g, carry):
            for r in range(SUBLANES):
                start_row(g * SUBLANES + r)
            return carry

        lax.fori_loop(0, ns_next * (sb // SUBLANES), issue, 0)


def _moe(tile_e, tile_ns, slot_tok3, xq, wgu, bgu, wd, bd3):
    d = wgu.shape[1]
    f = wd.shape[1]
    half = d // 2
    nt, _, tr = slot_tok3.shape
    nf = f // FC
    bgu3 = bgu.reshape(bgu.shape[0], 2 * nf, FC)

    def cur(i):
        return jnp.maximum(i - 1, 0)

    grid_spec = pltpu.PrefetchScalarGridSpec(
        num_scalar_prefetch=2,
        grid=(nt + 1,),
        in_specs=[
            pl.BlockSpec((1, 1, tr), lambda i, te, ns: (jnp.minimum(i, nt - 1), 0, 0),
                         memory_space=pltpu.SMEM),
            pl.BlockSpec(memory_space=pl.ANY),
            pl.BlockSpec(memory_space=pl.ANY),
            pl.BlockSpec(memory_space=pl.ANY),
            pl.BlockSpec((1, 2 * nf, FC), lambda i, te, ns: (te[cur(i)], 0, 0)),
            pl.BlockSpec((1, 1, d), lambda i, te, ns: (te[cur(i)], 0, 0)),
        ],
        out_specs=pl.BlockSpec((tr, d), lambda i, te, ns: (cur(i), 0)),
        scratch_shapes=[
            pltpu.VMEM((2, tr * (half // LANES), LANES), jnp.uint32),
            pltpu.VMEM((tr, d), BF16),
            pltpu.VMEM((2, d, FC), F32),
            pltpu.VMEM((2, d, FC), F32),
            pltpu.VMEM((2, FC, d), F32),
            pltpu.SemaphoreType.DMA((2,)),
            pltpu.SemaphoreType.DMA((2, 3)),
        ],
    )
    return pl.pallas_call(
        _moe_kernel,
        grid_spec=grid_spec,
        out_shape=jax.ShapeDtypeStruct((nt * tr, d), F32),
        compiler_params=pltpu.CompilerParams(
            dimension_semantics=("arbitrary",), vmem_limit_bytes=VMEM_LIMIT),
        name="moe",
    )(tile_e, tile_ns, slot_tok3, xq, wgu, wd, bgu3, bd3)


def _combine_kernel(dst_ref, h_ref, ys_hbm, rt_ref, p_ref, gple_ref, wpg_ref, wpp_ref, gfin_ref,
                    o_ref, ybuf, sem):
    tm = h_ref.shape[0]

    def issue(r, carry):
        for k in range(TOP_K):
            s = dst_ref[0, 0, r * TOP_K + k]
            pltpu.make_async_copy(ys_hbm.at[pl.ds(s, 1)], ybuf.at[k, pl.ds(r, 1)], sem).start(priority=k % 2)
        return carry

    lax.fori_loop(0, tm, issue, 0, unroll=4)
    for k in range(TOP_K):
        pltpu.make_async_copy(ys_hbm.at[pl.ds(0, tm)], ybuf.at[k], sem).wait()

    rt = rt_ref[...]
    y = rt[:, TOP_K:TOP_K + 1] * ybuf[0]
    for k in range(1, TOP_K):
        y = y + rt[:, TOP_K + k:TOP_K + k + 1] * ybuf[k]
    h2 = h_ref[...] + y
    gate = jax.nn.sigmoid(_dot(_rms(h2, gple_ref[...]).astype(BF16), wpg_ref[...]))
    pp = _dot(p_ref[...].astype(BF16), wpp_ref[...])
    h3 = h2 + gate * pp
    o_ref[...] = _rms(h3, gfin_ref[...])


def _combine(dest3, h, ys, rt, p2, gple, wpg, wpp, gfin):
    n, d = h.shape
    tm = TM_OUT
    pd = p2.shape[1]
    row = lambda i: (i, 0)
    const = lambda i: (0, 0)
    return pl.pallas_call(
        _combine_kernel,
        grid=(n // tm,),
        in_specs=[
            pl.BlockSpec((1, 1, tm * TOP_K), lambda i: (i, 0, 0), memory_space=pltpu.SMEM),
            pl.BlockSpec((tm, d), row),
            pl.BlockSpec(memory_space=pl.ANY),
            pl.BlockSpec((tm, LANES), row),
            pl.BlockSpec((tm, pd), row),
            pl.BlockSpec((1, d), const),
            pl.BlockSpec((d, d), const),
            pl.BlockSpec((pd, d), const),
            pl.BlockSpec((1, d), const),
        ],
        out_specs=pl.BlockSpec((tm, d), row),
        out_shape=jax.ShapeDtypeStruct((n, d), F32),
        scratch_shapes=[
            pltpu.VMEM((TOP_K, tm, d), F32),
            pltpu.SemaphoreType.DMA(()),
        ],
        compiler_params=pltpu.CompilerParams(
            dimension_semantics=("arbitrary",), vmem_limit_bytes=VMEM_LIMIT),
        name="combine",
    )(dest3, h, ys, rt, p2, gple, wpg, wpp, gfin)


def _layer(h2, p2, norm_mix, w_in, conv_a_w, w_a_out, conv_b_w, conv_b_b, w_rg, b_rg, w_ig, b_ig,
           lru_lambda, w_b_out, w_o, norm_ffn, w_router, b_router, w_gu, b_gu, w_down, b_down,
           norm_ple, w_ple_gate, w_ple_proj, norm_out):
    n, d = h2.shape
    nch = LRU_HEADS
    cw = d // nch

    w5 = w_in[:, :5 * d].reshape(d, 5, nch, cw).transpose(2, 0, 1, 3).reshape(nch, d, 5 * cw).astype(BF16)
    wg2 = w_in[:, 5 * d:].reshape(d, 2, nch, cw).transpose(2, 0, 1, 3).reshape(nch, d, 2 * cw).astype(BF16)
    wa = w_a_out.reshape(d, nch, cw).transpose(1, 0, 2).astype(BF16)
    wb = w_b_out.reshape(d, nch, cw).transpose(1, 0, 2).astype(BF16)
    wo = w_o.reshape(nch, cw, d).astype(BF16)
    wgate = jnp.concatenate([w_rg, w_ig], axis=-1).astype(BF16)
    par = jnp.concatenate([conv_a_w, conv_b_w, conv_b_b[None], b_rg[None], b_ig[None],
                           lru_lambda[None], jnp.zeros((16 - 11, d), F32)], axis=0)
    wr_pad = jnp.pad(w_router, ((0, 0), (0, LANES - N_EXPERTS)))
    wrh = wr_pad.astype(BF16)
    wrl = (wr_pad - wrh.astype(F32)).astype(BF16)
    br = jnp.pad(b_router, (0, LANES - N_EXPERTS))[None]

    a, yc, hy = _mix1(h2, norm_mix[None], w5, par, wgate)
    h_mid, xq, rt, cnt = _mix2(h2, a, yc, hy, wg2, wa, wb, wo, norm_ffn[None], wrh, wrl, br)

    top_idx = rt[:, 0:TOP_K].astype(jnp.int32)
    rank = rt[:, 2 * TOP_K:3 * TOP_K].astype(jnp.int32)
    counts = cnt[0, :N_EXPERTS].astype(jnp.int32)
    sb = TR // 2
    nt = -(-(n * TOP_K) // TR) + N_EXPERTS
    n_sub = (counts + sb - 1) // sb
    n_tile = (n_sub + 1) // 2
    t_ends = jnp.cumsum(n_tile)
    t_starts = t_ends - n_tile
    n_tiles = t_ends[-1]
    dest = (t_starts * TR)[top_idx] + rank
    tok = jnp.broadcast_to(jnp.arange(n, dtype=jnp.int32)[:, None], (n, TOP_K))
    slot_tok = jnp.zeros((nt * TR,), jnp.int32).at[dest.reshape(-1)].set(tok.reshape(-1))
    ti = jnp.arange(nt, dtype=jnp.int32)
    e_of = jnp.minimum(jnp.sum((t_ends[None, :] <= ti[:, None]).astype(jnp.int32), axis=1), N_EXPERTS - 1)
    tile_ns = jnp.where(ti < n_tiles, jnp.clip(n_sub[e_of] - 2 * (ti - t_starts[e_of]), 0, 2), 0)
    tile_e = jnp.where(ti < n_tiles, e_of, e_of[n_tiles - 1])

    ys = _moe(tile_e.astype(jnp.int32), tile_ns.astype(jnp.int32), slot_tok.reshape(nt, 1, TR), xq,
              w_gu, b_gu, w_down, b_down[:, None, :])

    out = _combine(dest.reshape(n // TM_OUT, 1, TM_OUT * TOP_K), h_mid, ys, rt, p2, norm_ple[None],
                   w_ple_gate.astype(BF16), w_ple_proj.astype(BF16), norm_out[None])
    return out


def kernel(x, p, norm_mix, w_in, conv_a_w, w_a_out, conv_b_w, conv_b_b, w_rg, b_rg, w_ig, b_ig, lru_lambda, w_b_out, w_o, norm_ffn, w_router, b_router, w_gu, b_gu, w_down, b_down, norm_ple, w_ple_gate, w_ple_proj, norm_final):
    bsz, t, d = x.shape
    depth = p.shape[0]
    assert bsz == 1 and depth == 1, "kernel fuses the final norm into the single layer"
    out = _layer(x.reshape(t, d), p[0].reshape(t, -1), norm_mix[0], w_in[0], conv_a_w[0], w_a_out[0],
                 conv_b_w[0], conv_b_b[0], w_rg[0], b_rg[0], w_ig[0], b_ig[0], lru_lambda[0], w_b_out[0],
                 w_o[0], norm_ffn[0], w_router[0], b_router[0], w_gu[0], b_gu[0], w_down[0], b_down[0],
                 norm_ple[0], w_ple_gate[0], w_ple_proj[0], norm_final)
    return out.reshape(bsz, t, d)
```

```python
import functools

import jax
import jax.numpy as jnp
from jax import lax
from jax.experimental import pallas as pl
from jax.experimental.pallas import tpu as pltpu

F32 = jnp.float32
BF16 = jnp.bfloat16

NORM_EPS = 1e-6
LRU_HEADS = 8
LRU_C = 8.0
N_EXPERTS = 32
TOP_K = 4
SWIGLU_LIMIT = 7.0
SWIGLU_ALPHA = 1.702

LANES = 128
SUBLANES = 8
VMEM_LIMIT = 56 * 1024 * 1024

TM_MIX = 512
TM_OUT = 256
TR = 1024
FC = 256
MOE_CHAIN = 256
MIX2_CHAIN = 256
NEG_BIG = -3.0e38


def _rms(x, g):
    ms = jnp.mean(x * x, axis=-1, keepdims=True)
    return x * lax.rsqrt(ms + NORM_EPS) * g


def _dot(a, b):
    return jnp.dot(a, b, preferred_element_type=F32)


def _shift_rows(u, hist8, k):
    rolled = pltpu.roll(u, k, axis=0)
    hr = pltpu.roll(hist8, k, axis=0)
    row = lax.broadcasted_iota(jnp.int32, hist8.shape, 0)
    first = jnp.where(row < k, hr, rolled[:SUBLANES])
    return jnp.concatenate([first, rolled[SUBLANES:]], axis=0)


def _gelu_tanh(x):
    return 0.5 * x * (1.0 + jnp.tanh(0.7978845608028654 * (x + 0.044715 * (x * x * x))))


def _mix1_kernel(x_ref, gmix_ref, w5_ref, parn_ref, par_ref, wgate_ref,
                 a_ref, yc_ref, hy_ref,
                 hu_s, hx_s, hp_s, h_s, proj_a, proj_b, xc_a, xc_b, gz_s):
    i = pl.program_id(0)
    c = pl.program_id(1)
    nch = pl.num_programs(1) - 1
    tm, cw = yc_ref.shape

    @pl.when(jnp.logical_and(i == 0, c == 0))
    def _():
        hu_s[...] = jnp.zeros_like(hu_s)
        hx_s[...] = jnp.zeros_like(hx_s)
        hp_s[...] = jnp.zeros_like(hp_s)

    @pl.when(c == 0)
    def _():
        a_ref[...] = _rms(x_ref[...], gmix_ref[...]).astype(BF16)

    def gates(xc_src):
        gz_s[...] = _dot(xc_src[...].astype(BF16), wgate_ref[0])

    def project(dst):
        p = _dot(a_ref[...], w5_ref[0])
        dst[...] = p
        return p[:, 4 * cw:5 * cw]

    def conv4(x_r, xc_dst):
        par = parn_ref[...]
        hx = hx_s[c]
        xc_dst[...] = (par[3:4] * _shift_rows(x_r, hx, 3) + par[4:5] * _shift_rows(x_r, hx, 2)
                       + par[5:6] * _shift_rows(x_r, hx, 1) + par[6:7] * x_r) + par[7:8]
        hx_s[c] = x_r[tm - SUBLANES:, :]

    def mix(src, xc_src):
        ch = c - 1
        b_c = src[:, 0 * cw:1 * cw]
        c_c = src[:, 1 * cw:2 * cw]
        v_c = src[:, 2 * cw:3 * cw]
        y_r = src[:, 3 * cw:4 * cw]
        xc = xc_src[...]
        par = par_ref[...]
        trow = lax.broadcasted_iota(jnp.int32, (tm, cw), 0)
        sub = trow % SUBLANES

        u = c_c * v_c
        hu = hu_s[ch]
        conv = (par[0:1] * _shift_rows(u, hu, 2) + par[1:2] * _shift_rows(u, hu, 1) + par[2:3] * u)
        yc_ref[...] = (b_c * conv).astype(BF16)
        hu_s[ch] = u[tm - SUBLANES:, :]

        r = jax.nn.sigmoid(gz_s[:, :cw] + par[8:9])
        ig = jax.nn.sigmoid(gz_s[:, cw:] + par[9:10])
        nlam = -par[10:11]
        softplus = jnp.maximum(nlam, 0.0) + jnp.log1p(jnp.exp(-jnp.abs(nlam)))
        log_a = (-LRU_C) * r * softplus
        a = jnp.exp(log_a)
        mult = jnp.sqrt(1.0 - a * a)
        mult = jnp.where(trow + i * tm == 0, 1.0, mult)
        b = xc * ig * mult

        for s in (1, 2, 4):
            a_sh = pltpu.roll(a, s, axis=0)
            b_sh = pltpu.roll(b, s, axis=0)
            m = sub >= s
            b = jnp.where(m, a * b_sh + b, b)
            a = jnp.where(m, a * a_sh, a)
        carry = hp_s[ch][SUBLANES - 1:SUBLANES, :]
        for g in range(tm // SUBLANES):
            lo = g * SUBLANES
            hg = b[lo:lo + SUBLANES, :] + a[lo:lo + SUBLANES, :] * carry
            h_s[lo:lo + SUBLANES, :] = hg
            carry = hg[SUBLANES - 1:SUBLANES, :]
        hp_s[ch] = h_s[tm - SUBLANES:, :]
        hy_ref[...] = (h_s[...] * _gelu_tanh(y_r)).astype(BF16)

    def step(src, dst):
        if src is not None:
            gates(src[1])
        x_r = project(dst[0]) if dst is not None else None
        if src is not None:
            mix(src[0], src[1])
        if dst is not None:
            conv4(x_r, dst[1])

    buf_a, buf_b = (proj_a, xc_a), (proj_b, xc_b)

    @pl.when(c == 0)
    def _():
        step(None, buf_a)

    middle = jnp.logical_and(c > 0, c < nch)

    @pl.when(jnp.logical_and(middle, c % 2 == 1))
    def _():
        step(buf_a, buf_b)

    @pl.when(jnp.logical_and(middle, c % 2 == 0))
    def _():
        step(buf_b, buf_a)

    @pl.when(c == nch)
    def _():
        step(buf_b if (LRU_HEADS - 1) % 2 else buf_a, None)


def _mix1(x2, gmix, w5, par, wgate):
    n, d = x2.shape
    nch, _, w5c = w5.shape
    assert nch == LRU_HEADS
    cw = w5c // 5
    tm = TM_MIX
    grid = (n // tm, nch + 1)
    new_idx = lambda c: jnp.minimum(c, nch - 1)
    mix_idx = lambda c: jnp.maximum(c - 1, 0)
    return pl.pallas_call(
        _mix1_kernel,
        grid=grid,
        in_specs=[
            pl.BlockSpec((tm, d), lambda i, c: (i, 0)),
            pl.BlockSpec((1, d), lambda i, c: (0, 0)),
            pl.BlockSpec((1, d, w5c), lambda i, c: (new_idx(c), 0, 0)),
            pl.BlockSpec((16, cw), lambda i, c: (0, new_idx(c))),
            pl.BlockSpec((16, cw), lambda i, c: (0, mix_idx(c))),
            pl.BlockSpec((1, cw, 2 * cw), lambda i, c: (mix_idx(c), 0, 0)),
        ],
        out_specs=[
            pl.BlockSpec((tm, d), lambda i, c: (i, 0)),
            pl.BlockSpec((tm, cw), lambda i, c: (i, mix_idx(c))),
            pl.BlockSpec((tm, cw), lambda i, c: (i, mix_idx(c))),
        ],
        out_shape=[
            jax.ShapeDtypeStruct((n, d), BF16),
            jax.ShapeDtypeStruct((n, d), BF16),
            jax.ShapeDtypeStruct((n, d), BF16),
        ],
        scratch_shapes=[
            pltpu.VMEM((nch, SUBLANES, cw), F32),
            pltpu.VMEM((nch, SUBLANES, cw), F32),
            pltpu.VMEM((nch, SUBLANES, cw), F32),
            pltpu.VMEM((tm, cw), F32),
            pltpu.VMEM((tm, w5c), F32),
            pltpu.VMEM((tm, w5c), F32),
            pltpu.VMEM((tm, cw), F32),
            pltpu.VMEM((tm, cw), F32),
            pltpu.VMEM((tm, 2 * cw), F32),
        ],
        compiler_params=pltpu.CompilerParams(
            dimension_semantics=("arbitrary", "arbitrary"), vmem_limit_bytes=VMEM_LIMIT),
        name="mix1",
    )(x2, gmix, w5, par, par, wgate)


def _mix2_kernel(x_ref, a_ref, yc_ref, hy_ref, wg2_ref, wa_ref, wb_ref, wo_ref,
                 gffn_ref, wrh_ref, wrl_ref, br_ref,
                 h_ref, xq_ref, rt_ref, cnt_ref,
                 cnt_s):
    i = pl.program_id(0)
    kc = pl.program_id(1)
    nkc = pl.num_programs(1)
    tm = x_ref.shape[0]
    cw = wa_ref.shape[2]

    @pl.when(jnp.logical_and(i == 0, kc == 0))
    def _():
        cnt_s[...] = jnp.zeros_like(cnt_s)

    @pl.when(kc == 0)
    def _():
        h_ref[...] = x_ref[...]

    wg2 = wg2_ref[0]
    wa = wa_ref[0]
    wb = wb_ref[0]
    wo = wo_ref[0]
    chains = [slice(c * MIX2_CHAIN, (c + 1) * MIX2_CHAIN) for c in range(tm // MIX2_CHAIN)]
    pre = []
    for rs in chains:
        g2 = _dot(a_ref[rs, :], wg2)
        y_conv = _dot(yc_ref[rs, :], wa)
        y_rec = _dot(hy_ref[rs, :], wb)
        pre.append((g2, y_conv, y_rec))
    for rs, (g2, y_conv, y_rec) in zip(chains, pre):
        merged = jax.nn.sigmoid(g2[:, :cw]) * y_conv + jax.nn.sigmoid(g2[:, cw:]) * y_rec
        h_ref[rs, :] += _dot(merged.astype(BF16), wo)

    @pl.when(kc == nkc - 1)
    def _():
        xn = _rms(h_ref[...], gffn_ref[...])
        xh = xn.astype(BF16)
        xhf = xh.astype(F32)
        xl = (xn - xhf).astype(BF16)
        half = xn.shape[1] // 2
        lo = pltpu.bitcast(xhf[:, :half], jnp.uint32) >> 16
        hi = pltpu.bitcast(xhf[:, half:], jnp.uint32) & jnp.uint32(0xFFFF0000)
        words = lo | hi
        tile_rows = half // LANES
        for q in range(tile_rows):
            xq_ref[pl.ds(q, tm, stride=tile_rows), :] = words[:, q * LANES:(q + 1) * LANES]
        wrh = wrh_ref[...]
        logits = _dot(xh, wrh) + _dot(xh, wrl_ref[...]) + _dot(xl, wrh) + br_ref[...]
        lane = lax.broadcasted_iota(jnp.int32, (tm, LANES), 1)
        lane_f = lane.astype(F32)
        work = jnp.where(lane < N_EXPERTS, logits, NEG_BIG)
        vals, idxs, hots = [], [], []
        for _k in range(TOP_K):
            m = jnp.max(work, axis=-1, keepdims=True)
            idx = jnp.min(jnp.where(work == m, lane_f, float(LANES)), axis=-1, keepdims=True)
            hot = lane_f == idx
            work = jnp.where(hot, NEG_BIG, work)
            vals.append(m)
            idxs.append(idx)
            hots.append(hot)
        es = [jnp.exp(v - vals[0]) for v in vals]
        denom = es[0] + es[1] + es[2] + es[3]
        ws = [e / denom for e in es]
        hot_any = jnp.zeros((tm, LANES), F32)
        for hot in hots:
            hot_any = hot_any + hot.astype(F32)
        rr = lax.broadcasted_iota(jnp.int32, (tm, tm), 0)
        cc = lax.broadcasted_iota(jnp.int32, (tm, tm), 1)
        lower = jnp.where(rr > cc, 1.0, 0.0).astype(BF16)
        cum = _dot(lower, hot_any.astype(BF16)) + cnt_s[0:1, :]
        out = jnp.zeros((tm, LANES), F32)
        for k in range(TOP_K):
            rank = jnp.sum(jnp.where(hots[k], cum, 0.0), axis=-1, keepdims=True)
            out = jnp.where(lane == k, idxs[k], out)
            out = jnp.where(lane == TOP_K + k, ws[k], out)
            out = jnp.where(lane == 2 * TOP_K + k, rank, out)
        rt_ref[...] = out
        cnt_new = cnt_s[...] + jnp.sum(hot_any, axis=0, keepdims=True)
        cnt_s[...] = cnt_new
        cnt_ref[...] = cnt_new


def _mix2(x2, a, yc, hy, wg2, wa, wb, wo, gffn, wrh, wrl, br):
    n, d = x2.shape
    nkc, _, cw = wa.shape
    tm = TM_MIX
    grid = (n // tm, nkc)
    row = lambda i, k: (i, 0)
    const = lambda i, k: (0, 0)
    chunk = lambda i, k: (k, 0, 0)
    return pl.pallas_call(
        _mix2_kernel,
        grid=grid,
        in_specs=[
            pl.BlockSpec((tm, d), row),
            pl.BlockSpec((tm, d), row),
            pl.BlockSpec((tm, d), row),
            pl.BlockSpec((tm, d), row),
            pl.BlockSpec((1, d, 2 * cw), chunk),
            pl.BlockSpec((1, d, cw), chunk),
            pl.BlockSpec((1, d, cw), chunk),
            pl.BlockSpec((1, cw, d), chunk),
            pl.BlockSpec((1, d), const),
            pl.BlockSpec((d, LANES), const),
            pl.BlockSpec((d, LANES), const),
            pl.BlockSpec((1, LANES), const),
        ],
        out_specs=[
            pl.BlockSpec((tm, d), row),
            pl.BlockSpec((tm * (d // 2 // LANES), LANES), row),
            pl.BlockSpec((tm, LANES), row),
            pl.BlockSpec((SUBLANES, LANES), const),
        ],
        out_shape=[
            jax.ShapeDtypeStruct((n, d), F32),
            jax.ShapeDtypeStruct((n * (d // 2 // LANES), LANES), jnp.uint32),
            jax.ShapeDtypeStruct((n, LANES), F32),
            jax.ShapeDtypeStruct((SUBLANES, LANES), F32),
        ],
        scratch_shapes=[pltpu.VMEM((SUBLANES, LANES), F32)],
        compiler_params=pltpu.CompilerParams(
            dimension_semantics=("arbitrary", "arbitrary"), vmem_limit_bytes=VMEM_LIMIT),
        name="mix2",
    )(x2, a, yc, hy, wg2, wa, wb, wo, gffn, wrh, wrl, br)


def _moe_kernel(te_ref, ns_ref,
                tok_ref, xq_hbm, wgu_hbm, wd_hbm, bgu_ref, bd_ref,
                o_ref, gbuf, xb, wg_s, wu_s, wd_s, gsem, wsem):
    i = pl.program_id(0)
    nt = pl.num_programs(0) - 1
    tr, d = o_ref.shape
    sb = tr // 2
    fc = wg_s.shape[2]
    f = wd_hbm.shape[1]
    nf = f // fc
    half = d // 2
    tile_rows = half // LANES
    per_chunk = tr // (nf // 2)

    i_n = jnp.minimum(i, nt - 1)
    ns_next = jnp.where(i < nt, ns_ref[i_n], 0)
    slot_n = i % 2
    issue_iters = ns_next * (sb // per_chunk)
    ic = jnp.maximum(i - 1, 0)
    ns_cur = jnp.where(i > 0, ns_ref[ic], 0)
    slot_c = ic % 2
    e_cur = te_ref[ic]

    def start_row(r):
        src = pl.multiple_of(tok_ref[0, 0, r] * tile_rows, tile_rows)
        dst = pl.multiple_of(r * tile_rows, tile_rows)
        pltpu.make_async_copy(xq_hbm.at[pl.ds(src, tile_rows)], gbuf.at[slot_n, pl.ds(dst, tile_rows)],
                              gsem.at[slot_n]).start()

    def weight_copies(e, j, slot):
        col = pl.multiple_of(j * fc, fc)
        ucol = pl.multiple_of(f + j * fc, fc)
        return (
            pltpu.make_async_copy(wgu_hbm.at[e, :, pl.ds(col, fc)], wg_s.at[slot], wsem.at[slot, 0]),
            pltpu.make_async_copy(wgu_hbm.at[e, :, pl.ds(ucol, fc)], wu_s.at[slot], wsem.at[slot, 1]),
            pltpu.make_async_copy(wd_hbm.at[e, pl.ds(col, fc), :], wd_s.at[slot], wsem.at[slot, 2]),
        )

    @pl.when(jnp.logical_and(i == 0, ns_next > 0))
    def _():
        for cp in weight_copies(te_ref[0], 0, 0):
            cp.start()

    @pl.when(ns_cur > 0)
    def _():
        rows = ns_cur * (sb * tile_rows)
        pltpu.make_async_copy(xq_hbm.at[pl.ds(0, rows)], gbuf.at[slot_c, pl.ds(0, rows)],
                              gsem.at[slot_c]).wait()

    for sub in range(2):
        lo_r, hi_r = sub * sb, (sub + 1) * sb

        @pl.when(ns_cur > sub)
        def _(lo_r=lo_r, hi_r=hi_r):
            for q in range(tile_rows):
                w = gbuf[slot_c, pl.ds(lo_r * tile_rows + q, sb, stride=tile_rows), :]
                cl = slice(q * LANES, (q + 1) * LANES)
                ch = slice(half + q * LANES, half + (q + 1) * LANES)
                xb[lo_r:hi_r, cl] = pltpu.bitcast(w << 16, F32).astype(BF16)
                xb[lo_r:hi_r, ch] = pltpu.bitcast(w & jnp.uint32(0xFFFF0000), F32).astype(BF16)
            o_ref[lo_r:hi_r, :] = jnp.broadcast_to(bd_ref[0], (sb, d))

        @pl.when(ns_cur <= sub)
        def _(lo_r=lo_r, hi_r=hi_r):
            o_ref[lo_r:hi_r, :] = jnp.zeros((sb, d), o_ref.dtype)

    def chunk_body(j, nrows, with_issue):
        slot = j % 2
        for cp in weight_copies(e_cur, j, slot):
            cp.wait()
        last = j == nf - 1
        j_n = jnp.where(last, 0, j + 1)
        e_n = jnp.where(last, te_ref[i_n], e_cur)

        @pl.when(jnp.logical_or(jnp.logical_not(last), ns_next > 0))
        def _():
            for cp in weight_copies(e_n, j_n, 1 - slot):
                cp.start()

        if with_issue:
            for r in range(per_chunk):
                start_row(j * per_chunk + r)
        wgu = jnp.concatenate([wg_s[slot], wu_s[slot]], axis=1).astype(BF16)
        wdn = wd_s[slot].astype(BF16)
        bg = bgu_ref[0, pl.ds(j, 1), :]
        bu = bgu_ref[0, pl.ds(nf + j, 1), :]
        chain = min(MOE_CHAIN, nrows)
        chains = [slice(c * chain, (c + 1) * chain) for c in range(nrows // chain)]
        gus = [_dot(xb[rs, :], wgu) for rs in chains]
        for rs, gu in zip(chains, gus):
            gate = jnp.minimum(gu[:, :fc] + bg, SWIGLU_LIMIT)
            up = jnp.clip(gu[:, fc:] + bu, -SWIGLU_LIMIT, SWIGLU_LIMIT)
            glu = gate * jax.nn.sigmoid(SWIGLU_ALPHA * gate)
            act = ((up + 1.0) * glu).astype(BF16)
            o_ref[rs, :] += _dot(act, wdn)

    for nsub in (1, 2):
        @pl.when(ns_cur == nsub)
        def _(nsub=nsub):
            def with_issue(j, carry):
                chunk_body(j, nsub * sb, True)
                return carry

            def without_issue(j, carry):
                chunk_body(j, nsub * sb, False)
                return carry

            lax.fori_loop(0, issue_iters, with_issue, 0)
            lax.fori_loop(issue_iters, nf, without_issue, 0)

    @pl.when(jnp.logical_and(ns_cur == 0, ns_next > 0))
    def _():
        def issue(g, carry):
            for r in range(SUBLANES):
                start_row(g * SUBLANES + r)
            return carry

        lax.fori_loop(0, ns_next * (sb // SUBLANES), issue, 0)


def _moe(tile_e, tile_ns, slot_tok3, xq, wgu, bgu, wd, bd3):
    d = wgu.shape[1]
    f = wd.shape[1]
    half = d // 2
    nt, _, tr = slot_tok3.shape
    nf = f // FC
    bgu3 = bgu.reshape(bgu.shape[0], 2 * nf, FC)

    def cur(i):
        return jnp.maximum(i - 1, 0)

    grid_spec = pltpu.PrefetchScalarGridSpec(
        num_scalar_prefetch=2,
        grid=(nt + 1,),
        in_specs=[
            pl.BlockSpec((1, 1, tr), lambda i, te, ns: (jnp.minimum(i, nt - 1), 0, 0),
                         memory_space=pltpu.SMEM),
            pl.BlockSpec(memory_space=pl.ANY),
            pl.BlockSpec(memory_space=pl.ANY),
            pl.BlockSpec(memory_space=pl.ANY),
            pl.BlockSpec((1, 2 * nf, FC), lambda i, te, ns: (te[cur(i)], 0, 0)),
            pl.BlockSpec((1, 1, d), lambda i, te, ns: (te[cur(i)], 0, 0)),
        ],
        out_specs=pl.BlockSpec((tr, d), lambda i, te, ns: (cur(i), 0)),
        scratch_shapes=[
            pltpu.VMEM((2, tr * (half // LANES), LANES), jnp.uint32),
            pltpu.VMEM((tr, d), BF16),
            pltpu.VMEM((2, d, FC), F32),
            pltpu.VMEM((2, d, FC), F32),
            pltpu.VMEM((2, FC, d), F32),
            pltpu.SemaphoreType.DMA((2,)),
            pltpu.SemaphoreType.DMA((2, 3)),
        ],
    )
    return pl.pallas_call(
        _moe_kernel,
        grid_spec=grid_spec,
        out_shape=jax.ShapeDtypeStruct((nt * tr, d), F32),
        compiler_params=pltpu.CompilerParams(
            dimension_semantics=("arbitrary",), vmem_limit_bytes=VMEM_LIMIT),
        name="moe",
    )(tile_e, tile_ns, slot_tok3, xq, wgu, wd, bgu3, bd3)


def _combine_kernel(dst_ref, h_ref, ys_hbm, rt_ref, p_ref, gple_ref, wpg_ref, wpp_ref, gfin_ref,
                    o_ref, ybuf, sem):
    tm = h_ref.shape[0]

    def issue(r, carry):
        for k in range(TOP_K):
            s = dst_ref[0, 0, r * TOP_K + k]
            pltpu.make_async_copy(ys_hbm.at[pl.ds(s, 1)], ybuf.at[k, pl.ds(r, 1)], sem).start(priority=k % 2)
        return carry

    lax.fori_loop(0, tm, issue, 0, unroll=4)
    for k in range(TOP_K):
        pltpu.make_async_copy(ys_hbm.at[pl.ds(0, tm)], ybuf.at[k], sem).wait()

    rt = rt_ref[...]
    y = rt[:, TOP_K:TOP_K + 1] * ybuf[0]
    for k in range(1, TOP_K):
        y = y + rt[:, TOP_K + k:TOP_K + k + 1] * ybuf[k]
    h2 = h_ref[...] + y
    gate = jax.nn.sigmoid(_dot(_rms(h2, gple_ref[...]).astype(BF16), wpg_ref[...]))
    pp = _dot(p_ref[...].astype(BF16), wpp_ref[...])
    h3 = h2 + gate * pp
    o_ref[...] = _rms(h3, gfin_ref[...])


def _combine(dest3, h, ys, rt, p2, gple, wpg, wpp, gfin):
    n, d = h.shape
    tm = TM_OUT
    pd = p2.shape[1]
    row = lambda i: (i, 0)
    const = lambda i: (0, 0)
    return pl.pallas_call(
        _combine_kernel,
        grid=(n // tm,),
        in_specs=[
            pl.BlockSpec((1, 1, tm * TOP_K), lambda i: (i, 0, 0), memory_space=pltpu.SMEM),
            pl.BlockSpec((tm, d), row),
            pl.BlockSpec(memory_space=pl.ANY),
            pl.BlockSpec((tm, LANES), row),
            pl.BlockSpec((tm, pd), row),
            pl.BlockSpec((1, d), const),
            pl.BlockSpec((d, d), const),
            pl.BlockSpec((pd, d), const),
            pl.BlockSpec((1, d), const),
        ],
        out_specs=pl.BlockSpec((tm, d), row),
        out_shape=jax.ShapeDtypeStruct((n, d), F32),
        scratch_shapes=[
            pltpu.VMEM((TOP_K, tm, d), F32),
            pltpu.SemaphoreType.DMA(()),
        ],
        compiler_params=pltpu.CompilerParams(
            dimension_semantics=("arbitrary",), vmem_limit_bytes=VMEM_LIMIT),
        name="combine",
    )(dest3, h, ys, rt, p2, gple, wpg, wpp, gfin)


def _layer(h2, p2, norm_mix, w_in, conv_a_w, w_a_out, conv_b_w, conv_b_b, w_rg, b_rg, w_ig, b_ig,
           lru_lambda, w_b_out, w_o, norm_ffn, w_router, b_router, w_gu, b_gu, w_down, b_down,
           norm_ple, w_ple_gate, w_ple_proj, norm_out):
    n, d = h2.shape
    nch = LRU_HEADS
    cw = d // nch

    w5 = w_in[:, :5 * d].reshape(d, 5, nch, cw).transpose(2, 0, 1, 3).reshape(nch, d, 5 * cw).astype(BF16)
    wg2 = w_in[:, 5 * d:].reshape(d, 2, nch, cw).transpose(2, 0, 1, 3).reshape(nch, d, 2 * cw).astype(BF16)
    wa = w_a_out.reshape(d, nch, cw).transpose(1, 0, 2).astype(BF16)
    wb = w_b_out.reshape(d, nch, cw).transpose(1, 0, 2).astype(BF16)
    wo = w_o.reshape(nch, cw, d).astype(BF16)
    wgate = jnp.concatenate([w_rg, w_ig], axis=-1).astype(BF16)
    par = jnp.concatenate([conv_a_w, conv_b_w, conv_b_b[None], b_rg[None], b_ig[None],
                           lru_lambda[None], jnp.zeros((16 - 11, d), F32)], axis=0)
    wr_pad = jnp.pad(w_router, ((0, 0), (0, LANES - N_EXPERTS)))
    wrh = wr_pad.astype(BF16)
    wrl = (wr_pad - wrh.astype(F32)).astype(BF16)
    br = jnp.pad(b_router, (0, LANES - N_EXPERTS))[None]

    a, yc, hy = _mix1(h2, norm_mix[None], w5, par, wgate)
    h_mid, xq, rt, cnt = _mix2(h2, a, yc, hy, wg2, wa, wb, wo, norm_ffn[None], wrh, wrl, br)

    top_idx = rt[:, 0:TOP_K].astype(jnp.int32)
    rank = rt[:, 2 * TOP_K:3 * TOP_K].astype(jnp.int32)
    counts = cnt[0, :N_EXPERTS].astype(jnp.int32)
    sb = TR // 2
    nt = -(-(n * TOP_K) // TR) + N_EXPERTS
    n_sub = (counts + sb - 1) // sb
    n_tile = (n_sub + 1) // 2
    t_ends = jnp.cumsum(n_tile)
    t_starts = t_ends - n_tile
    n_tiles = t_ends[-1]
    dest = (t_starts * TR)[top_idx] + rank
    tok = jnp.broadcast_to(jnp.arange(n, dtype=jnp.int32)[:, None], (n, TOP_K))
    slot_tok = jnp.zeros((nt * TR,), jnp.int32).at[dest.reshape(-1)].set(tok.reshape(-1))
    ti = jnp.arange(nt, dtype=jnp.int32)
    e_of = jnp.minimum(jnp.sum((t_ends[None, :] <= ti[:, None]).astype(jnp.int32), axis=1), N_EXPERTS - 1)
    tile_ns = jnp.where(ti < n_tiles, jnp.clip(n_sub[e_of] - 2 * (ti - t_starts[e_of]), 0, 2), 0)
    tile_e = jnp.where(ti < n_tiles, e_of, e_of[n_tiles - 1])

    ys = _moe(tile_e.astype(jnp.int32), tile_ns.astype(jnp.int32), slot_tok.reshape(nt, 1, TR), xq,
              w_gu, b_gu, w_down, b_down[:, None, :])

    out = _combine(dest.reshape(n // TM_OUT, 1, TM_OUT * TOP_K), h_mid, ys, rt, p2, norm_ple[None],
                   w_ple_gate.astype(BF16), w_ple_proj.astype(BF16), norm_out[None])
    return out


def kernel(x, p, norm_mix, w_in, conv_a_w, w_a_out, conv_b_w, conv_b_b, w_rg, b_rg, w_ig, b_ig, lru_lambda, w_b_out, w_o, norm_ffn, w_router, b_router, w_gu, b_gu, w_down, b_down, norm_ple, w_ple_gate, w_ple_proj, norm_final):
    bsz, t, d = x.shape
    depth = p.shape[0]
    assert bsz == 1 and depth == 1, "kernel fuses the final norm into the single layer"
    out = _layer(x.reshape(t, d), p[0].reshape(t, -1), norm_mix[0], w_in[0], conv_a_w[0], w_a_out[0],
                 conv_b_w[0], conv_b_b[0], w_rg[0], b_rg[0], w_ig[0], b_ig[0], lru_lambda[0], w_b_out[0],
                 w_o[0], norm_ffn[0], w_router[0], b_router[0], w_gu[0], b_gu[0], w_down[0], b_down[0],
                 norm_ple[0], w_ple_gate[0], w_ple_proj[0], norm_final)
    return out.reshape(bsz, t, d)
```

```python
import functools

import jax
import jax.numpy as jnp
from jax import lax
from jax.experimental import pallas as pl
from jax.experimental.pallas import tpu as pltpu

F32 = jnp.float32
BF16 = jnp.bfloat16

NORM_EPS = 1e-6
LRU_HEADS = 8
LRU_C = 8.0
N_EXPERTS = 32
TOP_K = 4
SWIGLU_LIMIT = 7.0
SWIGLU_ALPHA = 1.702

LANES = 128
SUBLANES = 8
VMEM_LIMIT = 56 * 1024 * 1024

TM_MIX = 512
TM_OUT = 256
TR = 1024
FC = 256
MIX1_CPS = 2
MOE_CHAIN = 256
MIX2_CHAIN = 256
NEG_BIG = -3.0e38


def _rms(x, g):
    ms = jnp.mean(x * x, axis=-1, keepdims=True)
    return x * lax.rsqrt(ms + NORM_EPS) * g


def _dot(a, b):
    return jnp.dot(a, b, preferred_element_type=F32)


def _shift_rows(u, hist8, k):
    rolled = pltpu.roll(u, k, axis=0)
    hr = pltpu.roll(hist8, k, axis=0)
    row = lax.broadcasted_iota(jnp.int32, hist8.shape, 0)
    first = jnp.where(row < k, hr, rolled[:SUBLANES])
    return jnp.concatenate([first, rolled[SUBLANES:]], axis=0)


def _gelu_tanh(x):
    return 0.5 * x * (1.0 + jnp.tanh(0.7978845608028654 * (x + 0.044715 * (x * x * x))))


def _mix1_kernel(x_ref, gmix_ref, wpb_ref, wpc_ref, wpv_ref, wpy_ref, wpx_ref, par_ref, wgate_ref,
                 a_ref, yc_ref, hy_ref,
                 hu_s, hx_s, hp_s, h_s):
    i = pl.program_id(0)
    c = pl.program_id(1)
    tm = yc_ref.shape[0]
    cps = wgate_ref.shape[0]
    cw = yc_ref.shape[1] // cps

    @pl.when(jnp.logical_and(i == 0, c == 0))
    def _():
        hu_s[...] = jnp.zeros_like(hu_s)
        hx_s[...] = jnp.zeros_like(hx_s)
        hp_s[...] = jnp.zeros_like(hp_s)

    @pl.when(c == 0)
    def _():
        a_ref[...] = _rms(x_ref[...], gmix_ref[...]).astype(BF16)

    trow = lax.broadcasted_iota(jnp.int32, (tm, cw), 0)
    sub = trow % SUBLANES
    first_step = trow + i * tm == 0

    for q in range(cps):
        ch = c * cps + q
        cols = slice(q * cw, (q + 1) * cw)
        a_bf = a_ref[...]
        x_r = _dot(a_bf, wpx_ref[:, cols])
        b_c = _dot(a_bf, wpb_ref[:, cols])
        c_c = _dot(a_bf, wpc_ref[:, cols])
        v_c = _dot(a_bf, wpv_ref[:, cols])
        y_r = _dot(a_bf, wpy_ref[:, cols])
        par = par_ref[:, cols]

        hx = hx_s[ch]
        xc = (par[3:4] * _shift_rows(x_r, hx, 3) + par[4:5] * _shift_rows(x_r, hx, 2)
              + par[5:6] * _shift_rows(x_r, hx, 1) + par[6:7] * x_r) + par[7:8]
        hx_s[ch] = x_r[tm - SUBLANES:, :]
        gz = _dot(xc.astype(BF16), wgate_ref[q])

        u = c_c * v_c
        hu = hu_s[ch]
        conv = (par[0:1] * _shift_rows(u, hu, 2) + par[1:2] * _shift_rows(u, hu, 1) + par[2:3] * u)
        yc_ref[:, cols] = (b_c * conv).astype(BF16)
        hu_s[ch] = u[tm - SUBLANES:, :]

        r = jax.nn.sigmoid(gz[:, :cw] + par[8:9])
        ig = jax.nn.sigmoid(gz[:, cw:] + par[9:10])
        nlam = -par[10:11]
        softplus = jnp.maximum(nlam, 0.0) + jnp.log1p(jnp.exp(-jnp.abs(nlam)))
        log_a = (-LRU_C) * r * softplus
        a = jnp.exp(log_a)
        mult = jnp.sqrt(1.0 - a * a)
        mult = jnp.where(first_step, 1.0, mult)
        b = xc * ig * mult

        for s in (1, 2, 4):
            a_sh = pltpu.roll(a, s, axis=0)
            b_sh = pltpu.roll(b, s, axis=0)
            m = sub >= s
            b = jnp.where(m, a * b_sh + b, b)
            a = jnp.where(m, a * a_sh, a)
        carry = hp_s[ch][SUBLANES - 1:SUBLANES, :]
        for g in range(tm // SUBLANES):
            lo = g * SUBLANES
            hg = b[lo:lo + SUBLANES, :] + a[lo:lo + SUBLANES, :] * carry
            h_s[q, lo:lo + SUBLANES, :] = hg
            carry = hg[SUBLANES - 1:SUBLANES, :]
        hp_s[ch] = h_s[q, tm - SUBLANES:, :]
        hy_ref[:, cols] = (h_s[q] * _gelu_tanh(y_r)).astype(BF16)


def _mix1(x2, gmix, w_in_bf, par, wgate):
    n, d = x2.shape
    nch, cw, _ = wgate.shape
    tm = TM_MIX
    cps = MIX1_CPS
    steps = nch // cps
    grid = (n // tm, steps)
    split = lambda s: pl.BlockSpec((d, cps * cw), lambda i, c: (0, s * steps + c))
    return pl.pallas_call(
        _mix1_kernel,
        grid=grid,
        in_specs=[
            pl.BlockSpec((tm, d), lambda i, c: (i, 0)),
            pl.BlockSpec((1, d), lambda i, c: (0, 0)),
            split(0), split(1), split(2), split(3), split(4),
            pl.BlockSpec((16, cps * cw), lambda i, c: (0, c)),
            pl.BlockSpec((cps, cw, 2 * cw), lambda i, c: (c, 0, 0)),
        ],
        out_specs=[
            pl.BlockSpec((tm, d), lambda i, c: (i, 0)),
            pl.BlockSpec((tm, cps * cw), lambda i, c: (i, c)),
            pl.BlockSpec((tm, cps * cw), lambda i, c: (i, c)),
        ],
        out_shape=[
            jax.ShapeDtypeStruct((n, d), BF16),
            jax.ShapeDtypeStruct((n, d), BF16),
            jax.ShapeDtypeStruct((n, d), BF16),
        ],
        scratch_shapes=[
            pltpu.VMEM((nch, SUBLANES, cw), F32),
            pltpu.VMEM((nch, SUBLANES, cw), F32),
            pltpu.VMEM((nch, SUBLANES, cw), F32),
            pltpu.VMEM((cps, tm, cw), F32),
        ],
        compiler_params=pltpu.CompilerParams(
            dimension_semantics=("arbitrary", "arbitrary"), vmem_limit_bytes=VMEM_LIMIT),
        name="mix1",
    )(x2, gmix, w_in_bf, w_in_bf, w_in_bf, w_in_bf, w_in_bf, par, wgate)


def _mix2_kernel(x_ref, a_ref, yc_ref, hy_ref, wgc_ref, wgr_ref, wa_ref, wb_ref, wo_ref,
                 gffn_ref, wrh_ref, wrl_ref, br_ref,
                 h_ref, xq_ref, rt_ref, cnt_ref,
                 cnt_s):
    i = pl.program_id(0)
    kc = pl.program_id(1)
    nkc = pl.num_programs(1)
    tm = x_ref.shape[0]
    cw = wa_ref.shape[1]

    @pl.when(jnp.logical_and(i == 0, kc == 0))
    def _():
        cnt_s[...] = jnp.zeros_like(cnt_s)

    @pl.when(kc == 0)
    def _():
        h_ref[...] = x_ref[...]

    wgc = wgc_ref[...]
    wgr = wgr_ref[...]
    wa = wa_ref[...]
    wb = wb_ref[...]
    wo = wo_ref[0]
    chains = [slice(c * MIX2_CHAIN, (c + 1) * MIX2_CHAIN) for c in range(tm // MIX2_CHAIN)]
    pre = []
    for rs in chains:
        g_conv = _dot(a_ref[rs, :], wgc)
        g_rec = _dot(a_ref[rs, :], wgr)
        y_conv = _dot(yc_ref[rs, :], wa)
        y_rec = _dot(hy_ref[rs, :], wb)
        pre.append((g_conv, g_rec, y_conv, y_rec))
    for rs, (g_conv, g_rec, y_conv, y_rec) in zip(chains, pre):
        merged = jax.nn.sigmoid(g_conv) * y_conv + jax.nn.sigmoid(g_rec) * y_rec
        h_ref[rs, :] += _dot(merged.astype(BF16), wo)

    @pl.when(kc == nkc - 1)
    def _():
        xn = _rms(h_ref[...], gffn_ref[...])
        xh = xn.astype(BF16)
        xhf = xh.astype(F32)
        xl = (xn - xhf).astype(BF16)
        half = xn.shape[1] // 2
        lo = pltpu.bitcast(xhf[:, :half], jnp.uint32) >> 16
        hi = pltpu.bitcast(xhf[:, half:], jnp.uint32) & jnp.uint32(0xFFFF0000)
        words = lo | hi
        tile_rows = half // LANES
        for q in range(tile_rows):
            xq_ref[pl.ds(q, tm, stride=tile_rows), :] = words[:, q * LANES:(q + 1) * LANES]
        wrh = wrh_ref[...]
        logits = _dot(xh, wrh) + _dot(xh, wrl_ref[...]) + _dot(xl, wrh) + br_ref[...]
        lane = lax.broadcasted_iota(jnp.int32, (tm, LANES), 1)
        lane_f = lane.astype(F32)
        work = jnp.where(lane < N_EXPERTS, logits, NEG_BIG)
        vals, idxs, hots = [], [], []
        for _k in range(TOP_K):
            m = jnp.max(work, axis=-1, keepdims=True)
            idx = jnp.min(jnp.where(work == m, lane_f, float(LANES)), axis=-1, keepdims=True)
            hot = lane_f == idx
            work = jnp.where(hot, NEG_BIG, work)
            vals.append(m)
            idxs.append(idx)
            hots.append(hot)
        es = [jnp.exp(v - vals[0]) for v in vals]
        denom = es[0] + es[1] + es[2] + es[3]
        ws = [e / denom for e in es]
        hot_any = jnp.zeros((tm, LANES), F32)
        for hot in hots:
            hot_any = hot_any + hot.astype(F32)
        rr = lax.broadcasted_iota(jnp.int32, (tm, tm), 0)
        cc = lax.broadcasted_iota(jnp.int32, (tm, tm), 1)
        lower = jnp.where(rr > cc, 1.0, 0.0).astype(BF16)
        cum = _dot(lower, hot_any.astype(BF16)) + cnt_s[0:1, :]
        out = jnp.zeros((tm, LANES), F32)
        for k in range(TOP_K):
            rank = jnp.sum(jnp.where(hots[k], cum, 0.0), axis=-1, keepdims=True)
            out = jnp.where(lane == k, idxs[k], out)
            out = jnp.where(lane == TOP_K + k, ws[k], out)
            out = jnp.where(lane == 2 * TOP_K + k, rank, out)
        rt_ref[...] = out
        cnt_new = cnt_s[...] + jnp.sum(hot_any, axis=0, keepdims=True)
        cnt_s[...] = cnt_new
        cnt_ref[...] = cnt_new


def _mix2(x2, a, yc, hy, w_in_bf, wa, wb, wo, gffn, wrh, wrl, br):
    n, d = x2.shape
    nkc, cw, _ = wo.shape
    tm = TM_MIX
    grid = (n // tm, nkc)
    row = lambda i, k: (i, 0)
    const = lambda i, k: (0, 0)
    chunk = lambda i, k: (k, 0, 0)
    return pl.pallas_call(
        _mix2_kernel,
        grid=grid,
        in_specs=[
            pl.BlockSpec((tm, d), row),
            pl.BlockSpec((tm, d), row),
            pl.BlockSpec((tm, d), row),
            pl.BlockSpec((tm, d), row),
            pl.BlockSpec((d, cw), lambda i, k: (0, 5 * nkc + k)),
            pl.BlockSpec((d, cw), lambda i, k: (0, 6 * nkc + k)),
            pl.BlockSpec((d, cw), lambda i, k: (0, k)),
            pl.BlockSpec((d, cw), lambda i, k: (0, k)),
            pl.BlockSpec((1, cw, d), chunk),
            pl.BlockSpec((1, d), const),
            pl.BlockSpec((d, LANES), const),
            pl.BlockSpec((d, LANES), const),
            pl.BlockSpec((1, LANES), const),
        ],
        out_specs=[
            pl.BlockSpec((tm, d), row),
            pl.BlockSpec((tm * (d // 2 // LANES), LANES), row),
            pl.BlockSpec((tm, LANES), row),
            pl.BlockSpec((SUBLANES, LANES), const),
        ],
        out_shape=[
            jax.ShapeDtypeStruct((n, d), F32),
            jax.ShapeDtypeStruct((n * (d // 2 // LANES), LANES), jnp.uint32),
            jax.ShapeDtypeStruct((n, LANES), F32),
            jax.ShapeDtypeStruct((SUBLANES, LANES), F32),
        ],
        scratch_shapes=[pltpu.VMEM((SUBLANES, LANES), F32)],
        compiler_params=pltpu.CompilerParams(
            dimension_semantics=("arbitrary", "arbitrary"), vmem_limit_bytes=VMEM_LIMIT),
        name="mix2",
    )(x2, a, yc, hy, w_in_bf, w_in_bf, wa, wb, wo, gffn, wrh, wrl, br)


def _moe_kernel(te_ref, ns_ref,
                tok_ref, xq_hbm, wgu_hbm, wd_hbm, bgu_ref, bd_ref,
                o_ref, gbuf, xb, wg_s, wu_s, wd_s, gsem, wsem):
    i = pl.program_id(0)
    nt = pl.num_programs(0) - 1
    tr, d = o_ref.shape
    sb = tr // 2
    fc = wg_s.shape[2]
    f = wd_hbm.shape[1]
    nf = f // fc
    half = d // 2
    tile_rows = half // LANES
    per_chunk = tr // (nf // 2)

    i_n = jnp.minimum(i, nt - 1)
    ns_next = jnp.where(i < nt, ns_ref[i_n], 0)
    slot_n = i % 2
    issue_iters = ns_next * (sb // per_chunk)
    ic = jnp.maximum(i - 1, 0)
    ns_cur = jnp.where(i > 0, ns_ref[ic], 0)
    slot_c = ic % 2
    e_cur = te_ref[ic]

    def start_row(r):
        src = pl.multiple_of(tok_ref[0, 0, r] * tile_rows, tile_rows)
        dst = pl.multiple_of(r * tile_rows, tile_rows)
        pltpu.make_async_copy(xq_hbm.at[pl.ds(src, tile_rows)], gbuf.at[slot_n, pl.ds(dst, tile_rows)],
                              gsem.at[slot_n]).start()

    def weight_copies(e, j, slot):
        col = pl.multiple_of(j * fc, fc)
        ucol = pl.multiple_of(f + j * fc, fc)
        return (
            pltpu.make_async_copy(wgu_hbm.at[e, :, pl.ds(col, fc)], wg_s.at[slot], wsem.at[slot, 0]),
            pltpu.make_async_copy(wgu_hbm.at[e, :, pl.ds(ucol, fc)], wu_s.at[slot], wsem.at[slot, 1]),
            pltpu.make_async_copy(wd_hbm.at[e, pl.ds(col, fc), :], wd_s.at[slot], wsem.at[slot, 2]),
        )

    @pl.when(jnp.logical_and(i == 0, ns_next > 0))
    def _():
        for cp in weight_copies(te_ref[0], 0, 0):
            cp.start()

    @pl.when(ns_cur > 0)
    def _():
        rows = ns_cur * (sb * tile_rows)
        pltpu.make_async_copy(xq_hbm.at[pl.ds(0, rows)], gbuf.at[slot_c, pl.ds(0, rows)],
                              gsem.at[slot_c]).wait()

    for sub in range(2):
        lo_r, hi_r = sub * sb, (sub + 1) * sb

        @pl.when(ns_cur > sub)
        def _(lo_r=lo_r, hi_r=hi_r):
            for q in range(tile_rows):
                w = gbuf[slot_c, pl.ds(lo_r * tile_rows + q, sb, stride=tile_rows), :]
                cl = slice(q * LANES, (q + 1) * LANES)
                ch = slice(half + q * LANES, half + (q + 1) * LANES)
                xb[lo_r:hi_r, cl] = pltpu.bitcast(w << 16, F32).astype(BF16)
                xb[lo_r:hi_r, ch] = pltpu.bitcast(w & jnp.uint32(0xFFFF0000), F32).astype(BF16)
            o_ref[lo_r:hi_r, :] = jnp.broadcast_to(bd_ref[0], (sb, d))

        @pl.when(ns_cur <= sub)
        def _(lo_r=lo_r, hi_r=hi_r):
            o_ref[lo_r:hi_r, :] = jnp.zeros((sb, d), o_ref.dtype)

    def chunk_body(j, nrows, with_issue):
        slot = j % 2
        for cp in weight_copies(e_cur, j, slot):
            cp.wait()
        last = j == nf - 1
        j_n = jnp.where(last, 0, j + 1)
        e_n = jnp.where(last, te_ref[i_n], e_cur)

        @pl.when(jnp.logical_or(jnp.logical_not(last), ns_next > 0))
        def _():
            for cp in weight_copies(e_n, j_n, 1 - slot):
                cp.start()

        if with_issue:
            for r in range(per_chunk):
                start_row(j * per_chunk + r)
        wgu = jnp.concatenate([wg_s[slot], wu_s[slot]], axis=1).astype(BF16)
        wdn = wd_s[slot].astype(BF16)
        bg = bgu_ref[0, pl.ds(j, 1), :]
        bu = bgu_ref[0, pl.ds(nf + j, 1), :]
        chain = min(MOE_CHAIN, nrows)
        chains = [slice(c * chain, (c + 1) * chain) for c in range(nrows // chain)]
        gus = [_dot(xb[rs, :], wgu) for rs in chains]
        for rs, gu in zip(chains, gus):
            gate = jnp.minimum(gu[:, :fc] + bg, SWIGLU_LIMIT)
            up = jnp.clip(gu[:, fc:] + bu, -SWIGLU_LIMIT, SWIGLU_LIMIT)
            glu = gate * jax.nn.sigmoid(SWIGLU_ALPHA * gate)
            act = ((up + 1.0) * glu).astype(BF16)
            o_ref[rs, :] += _dot(act, wdn)

    for nsub in (1, 2):
        @pl.when(ns_cur == nsub)
        def _(nsub=nsub):
            def with_issue(j, carry):
                chunk_body(j, nsub * sb, True)
                return carry

            def without_issue(j, carry):
                chunk_body(j, nsub * sb, False)
                return carry

            lax.fori_loop(0, issue_iters, with_issue, 0)
            lax.fori_loop(issue_iters, nf, without_issue, 0)

    @pl.when(jnp.logical_and(ns_cur == 0, ns_next > 0))
    def _():
        def issue(g, carry):
            for r in range(SUBLANES):
                start_row(g * SUBLANES + r)
            return carry

        lax.fori_loop(0, ns_next * (sb // SUBLANES), issue, 0)


def _moe(tile_e, tile_ns, slot_tok3, xq, wgu, bgu, wd, bd3):
    d = wgu.shape[1]
    f = wd.shape[1]
    half = d // 2
    nt, _, tr = slot_tok3.shape
    nf = f // FC
    bgu3 = bgu.reshape(bgu.shape[0], 2 * nf, FC)

    def cur(i):
        return jnp.maximum(i - 1, 0)

    grid_spec = pltpu.PrefetchScalarGridSpec(
        num_scalar_prefetch=2,
        grid=(nt + 1,),
        in_specs=[
            pl.BlockSpec((1, 1, tr), lambda i, te, ns: (jnp.minimum(i, nt - 1), 0, 0),
                         memory_space=pltpu.SMEM),
            pl.BlockSpec(memory_space=pl.ANY),
            pl.BlockSpec(memory_space=pl.ANY),
            pl.BlockSpec(memory_space=pl.ANY),
            pl.BlockSpec((1, 2 * nf, FC), lambda i, te, ns: (te[cur(i)], 0, 0)),
            pl.BlockSpec((1, 1, d), lambda i, te, ns: (te[cur(i)], 0, 0)),
        ],
        out_specs=pl.BlockSpec((tr, d), lambda i, te, ns: (cur(i), 0)),
        scratch_shapes=[
            pltpu.VMEM((2, tr * (half // LANES), LANES), jnp.uint32),
            pltpu.VMEM((tr, d), BF16),
            pltpu.VMEM((2, d, FC), F32),
            pltpu.VMEM((2, d, FC), F32),
            pltpu.VMEM((2, FC, d), F32),
            pltpu.SemaphoreType.DMA((2,)),
            pltpu.SemaphoreType.DMA((2, 3)),
        ],
    )
    return pl.pallas_call(
        _moe_kernel,
        grid_spec=grid_spec,
        out_shape=jax.ShapeDtypeStruct((nt * tr, d), F32),
        compiler_params=pltpu.CompilerParams(
            dimension_semantics=("arbitrary",), vmem_limit_bytes=VMEM_LIMIT),
        name="moe",
    )(tile_e, tile_ns, slot_tok3, xq, wgu, wd, bgu3, bd3)


def _combine_kernel(dst_ref, h_ref, ys_hbm, rt_ref, p_ref, gple_ref, wpg_ref, wpp_ref, gfin_ref,
                    o_ref, ybuf, sem):
    tm = h_ref.shape[0]

    def issue(r, carry):
        for k in range(TOP_K):
            s = dst_ref[0, 0, r * TOP_K + k]
            pltpu.make_async_copy(ys_hbm.at[pl.ds(s, 1)], ybuf.at[k, pl.ds(r, 1)], sem).start(priority=k % 2)
        return carry

    lax.fori_loop(0, tm, issue, 0, unroll=4)
    for k in range(TOP_K):
        pltpu.make_async_copy(ys_hbm.at[pl.ds(0, tm)], ybuf.at[k], sem).wait()

    rt = rt_ref[...]
    y = rt[:, TOP_K:TOP_K + 1] * ybuf[0]
    for k in range(1, TOP_K):
        y = y + rt[:, TOP_K + k:TOP_K + k + 1] * ybuf[k]
    h2 = h_ref[...] + y
    gate = jax.nn.sigmoid(_dot(_rms(h2, gple_ref[...]).astype(BF16), wpg_ref[...]))
    pp = _dot(p_ref[...].astype(BF16), wpp_ref[...])
    h3 = h2 + gate * pp
    o_ref[...] = _rms(h3, gfin_ref[...])


def _combine(dest3, h, ys, rt, p2, gple, wpg, wpp, gfin):
    n, d = h.shape
    tm = TM_OUT
    pd = p2.shape[1]
    row = lambda i: (i, 0)
    const = lambda i: (0, 0)
    return pl.pallas_call(
        _combine_kernel,
        grid=(n // tm,),
        in_specs=[
            pl.BlockSpec((1, 1, tm * TOP_K), lambda i: (i, 0, 0), memory_space=pltpu.SMEM),
            pl.BlockSpec((tm, d), row),
            pl.BlockSpec(memory_space=pl.ANY),
            pl.BlockSpec((tm, LANES), row),
            pl.BlockSpec((tm, pd), row),
            pl.BlockSpec((1, d), const),
            pl.BlockSpec((d, d), const),
            pl.BlockSpec((pd, d), const),
            pl.BlockSpec((1, d), const),
        ],
        out_specs=pl.BlockSpec((tm, d), row),
        out_shape=jax.ShapeDtypeStruct((n, d), F32),
        scratch_shapes=[
            pltpu.VMEM((TOP_K, tm, d), F32),
            pltpu.SemaphoreType.DMA(()),
        ],
        compiler_params=pltpu.CompilerParams(
            dimension_semantics=("arbitrary",), vmem_limit_bytes=VMEM_LIMIT),
        name="combine",
    )(dest3, h, ys, rt, p2, gple, wpg, wpp, gfin)


def _layer(h2, p2, norm_mix, w_in, conv_a_w, w_a_out, conv_b_w, conv_b_b, w_rg, b_rg, w_ig, b_ig,
           lru_lambda, w_b_out, w_o, norm_ffn, w_router, b_router, w_gu, b_gu, w_down, b_down,
           norm_ple, w_ple_gate, w_ple_proj, norm_out):
    n, d = h2.shape
    nch = LRU_HEADS
    cw = d // nch

    w_in_bf = w_in.astype(BF16)
    wa = w_a_out.astype(BF16)
    wb = w_b_out.astype(BF16)
    wo = w_o.reshape(nch, cw, d).astype(BF16)
    wgate = jnp.concatenate([w_rg, w_ig], axis=-1).astype(BF16)
    par = jnp.concatenate([conv_a_w, conv_b_w, conv_b_b[None], b_rg[None], b_ig[None],
                           lru_lambda[None], jnp.zeros((16 - 11, d), F32)], axis=0)
    wr_pad = jnp.pad(w_router, ((0, 0), (0, LANES - N_EXPERTS)))
    wrh = wr_pad.astype(BF16)
    wrl = (wr_pad - wrh.astype(F32)).astype(BF16)
    br = jnp.pad(b_router, (0, LANES - N_EXPERTS))[None]

    a, yc, hy = _mix1(h2, norm_mix[None], w_in_bf, par, wgate)
    h_mid, xq, rt, cnt = _mix2(h2, a, yc, hy, w_in_bf, wa, wb, wo, norm_ffn[None], wrh, wrl, br)

    top_idx = rt[:, 0:TOP_K].astype(jnp.int32)
    rank = rt[:, 2 * TOP_K:3 * TOP_K].astype(jnp.int32)
    counts = cnt[0, :N_EXPERTS].astype(jnp.int32)
    sb = TR // 2
    nt = -(-(n * TOP_K) // TR) + N_EXPERTS
    n_sub = (counts + sb - 1) // sb
    n_tile = (n_sub + 1) // 2
    t_ends = jnp.cumsum(n_tile)
    t_starts = t_ends - n_tile
    n_tiles = t_ends[-1]
    dest = (t_starts * TR)[top_idx] + rank
    tok = jnp.broadcast_to(jnp.arange(n, dtype=jnp.int32)[:, None], (n, TOP_K))
    slot_tok = jnp.zeros((nt * TR,), jnp.int32).at[dest.reshape(-1)].set(tok.reshape(-1))
    ti = jnp.arange(nt, dtype=jnp.int32)
    e_of = jnp.minimum(jnp.sum((t_ends[None, :] <= ti[:, None]).astype(jnp.int32), axis=1), N_EXPERTS - 1)
    tile_ns = jnp.where(ti < n_tiles, jnp.clip(n_sub[e_of] - 2 * (ti - t_starts[e_of]), 0, 2), 0)
    tile_e = jnp.where(ti < n_tiles, e_of, e_of[n_tiles - 1])

    ys = _moe(tile_e.astype(jnp.int32), tile_ns.astype(jnp.int32), slot_tok.reshape(nt, 1, TR), xq,
              w_gu, b_gu, w_down, b_down[:, None, :])

    out = _combine(dest.reshape(n // TM_OUT, 1, TM_OUT * TOP_K), h_mid, ys, rt, p2, norm_ple[None],
                   w_ple_gate.astype(BF16), w_ple_proj.astype(BF16), norm_out[None])
    return out


def kernel(x, p, norm_mix, w_in, conv_a_w, w_a_out, conv_b_w, conv_b_b, w_rg, b_rg, w_ig, b_ig, lru_lambda, w_b_out, w_o, norm_ffn, w_router, b_router, w_gu, b_gu, w_down, b_down, norm_ple, w_ple_gate, w_ple_proj, norm_final):
    bsz, t, d = x.shape
    depth = p.shape[0]
    assert bsz == 1 and depth == 1, "kernel fuses the final norm into the single layer"
    out = _layer(x.reshape(t, d), p[0].reshape(t, -1), norm_mix[0], w_in[0], conv_a_w[0], w_a_out[0],
                 conv_b_w[0], conv_b_b[0], w_rg[0], b_rg[0], w_ig[0], b_ig[0], lru_lambda[0], w_b_out[0],
                 w_o[0], norm_ffn[0], w_router[0], b_router[0], w_gu[0], b_gu[0], w_down[0], b_down[0],
                 norm_ple[0], w_ple_gate[0], w_ple_proj[0], norm_final)
    return out.reshape(bsz, t, d)
```

```python
import functools

import jax
import jax.numpy as jnp
from jax import lax
from jax.experimental import pallas as pl
from jax.experimental.pallas import tpu as pltpu

F32 = jnp.float32
BF16 = jnp.bfloat16

NORM_EPS = 1e-6
LRU_HEADS = 8
LRU_C = 8.0
N_EXPERTS = 32
TOP_K = 4
SWIGLU_LIMIT = 7.0
SWIGLU_ALPHA = 1.702

LANES = 128
SUBLANES = 8
VMEM_LIMIT = 56 * 1024 * 1024

TM_MIX = 512
TM_OUT = 256
TR = 1024
FC = 256
MIX1_CPS = 2
MOE_CHAIN = 256
MIX2_CHAIN = 256
NEG_BIG = -3.0e38


def _rms(x, g):
    ms = jnp.mean(x * x, axis=-1, keepdims=True)
    return x * lax.rsqrt(ms + NORM_EPS) * g


def _dot(a, b):
    return jnp.dot(a, b, preferred_element_type=F32)


def _shift_rows(u, hist8, k):
    rolled = pltpu.roll(u, k, axis=0)
    hr = pltpu.roll(hist8, k, axis=0)
    row = lax.broadcasted_iota(jnp.int32, hist8.shape, 0)
    first = jnp.where(row < k, hr, rolled[:SUBLANES])
    return jnp.concatenate([first, rolled[SUBLANES:]], axis=0)


def _gelu_tanh(x):
    return 0.5 * x * (1.0 + jnp.tanh(0.7978845608028654 * (x + 0.044715 * (x * x * x))))


def _mix1_kernel(x_ref, gmix_ref, wpb_ref, wpc_ref, wpv_ref, wpy_ref, wpx_ref, par_ref, wgate_ref,
                 a_ref, yc_ref, hy_ref,
                 hu_s, hx_s, hp_s, h_s):
    i = pl.program_id(0)
    c = pl.program_id(1)
    tm = yc_ref.shape[0]
    cps = wgate_ref.shape[0]
    cw = yc_ref.shape[1] // cps

    @pl.when(jnp.logical_and(i == 0, c == 0))
    def _():
        hu_s[...] = jnp.zeros_like(hu_s)
        hx_s[...] = jnp.zeros_like(hx_s)
        hp_s[...] = jnp.zeros_like(hp_s)

    @pl.when(c == 0)
    def _():
        a_ref[...] = _rms(x_ref[...], gmix_ref[...]).astype(BF16)

    trow = lax.broadcasted_iota(jnp.int32, (tm, cw), 0)
    sub = trow % SUBLANES
    first_step = trow + i * tm == 0

    for q in range(cps):
        ch = c * cps + q
        cols = slice(q * cw, (q + 1) * cw)
        a_bf = a_ref[...]
        x_r = _dot(a_bf, wpx_ref[:, cols])
        b_c = _dot(a_bf, wpb_ref[:, cols])
        c_c = _dot(a_bf, wpc_ref[:, cols])
        v_c = _dot(a_bf, wpv_ref[:, cols])
        y_r = _dot(a_bf, wpy_ref[:, cols])
        par = par_ref[:, cols]

        hx = hx_s[ch]
        xc = (par[3:4] * _shift_rows(x_r, hx, 3) + par[4:5] * _shift_rows(x_r, hx, 2)
              + par[5:6] * _shift_rows(x_r, hx, 1) + par[6:7] * x_r) + par[7:8]
        hx_s[ch] = x_r[tm - SUBLANES:, :]
        gz = _dot(xc.astype(BF16), wgate_ref[q])

        u = c_c * v_c
        hu = hu_s[ch]
        conv = (par[0:1] * _shift_rows(u, hu, 2) + par[1:2] * _shift_rows(u, hu, 1) + par[2:3] * u)
        yc_ref[:, cols] = (b_c * conv).astype(BF16)
        hu_s[ch] = u[tm - SUBLANES:, :]

        r = jax.nn.sigmoid(gz[:, :cw] + par[8:9])
        ig = jax.nn.sigmoid(gz[:, cw:] + par[9:10])
        nlam = -par[10:11]
        softplus = jnp.maximum(nlam, 0.0) + jnp.log1p(jnp.exp(-jnp.abs(nlam)))
        log_a = (-LRU_C) * r * softplus
        a = jnp.exp(log_a)
        mult = jnp.sqrt(1.0 - a * a)
        mult = jnp.where(first_step, 1.0, mult)
        b = xc * ig * mult

        for s in (1, 2, 4):
            a_sh = pltpu.roll(a, s, axis=0)
            b_sh = pltpu.roll(b, s, axis=0)
            m = sub >= s
            b = jnp.where(m, a * b_sh + b, b)
            a = jnp.where(m, a * a_sh, a)
        carry = hp_s[ch][SUBLANES - 1:SUBLANES, :]
        for g in range(tm // SUBLANES):
            lo = g * SUBLANES
            hg = b[lo:lo + SUBLANES, :] + a[lo:lo + SUBLANES, :] * carry
            h_s[q, lo:lo + SUBLANES, :] = hg
            carry = hg[SUBLANES - 1:SUBLANES, :]
        hp_s[ch] = h_s[q, tm - SUBLANES:, :]
        hy_ref[:, cols] = (h_s[q] * _gelu_tanh(y_r)).astype(BF16)


def _mix1(x2, gmix, w_in_bf, par, wgate):
    n, d = x2.shape
    nch, cw, _ = wgate.shape
    tm = TM_MIX
    cps = MIX1_CPS
    steps = nch // cps
    grid = (n // tm, steps)
    split = lambda s: pl.BlockSpec((d, cps * cw), lambda i, c: (0, s * steps + c))
    return pl.pallas_call(
        _mix1_kernel,
        grid=grid,
        in_specs=[
            pl.BlockSpec((tm, d), lambda i, c: (i, 0)),
            pl.BlockSpec((1, d), lambda i, c: (0, 0)),
            split(0), split(1), split(2), split(3), split(4),
            pl.BlockSpec((16, cps * cw), lambda i, c: (0, c)),
            pl.BlockSpec((cps, cw, 2 * cw), lambda i, c: (c, 0, 0)),
        ],
        out_specs=[
            pl.BlockSpec((tm, d), lambda i, c: (i, 0)),
            pl.BlockSpec((tm, cps * cw), lambda i, c: (i, c)),
            pl.BlockSpec((tm, cps * cw), lambda i, c: (i, c)),
        ],
        out_shape=[
            jax.ShapeDtypeStruct((n, d), BF16),
            jax.ShapeDtypeStruct((n, d), BF16),
            jax.ShapeDtypeStruct((n, d), BF16),
        ],
        scratch_shapes=[
            pltpu.VMEM((nch, SUBLANES, cw), F32),
            pltpu.VMEM((nch, SUBLANES, cw), F32),
            pltpu.VMEM((nch, SUBLANES, cw), F32),
            pltpu.VMEM((cps, tm, cw), F32),
        ],
        compiler_params=pltpu.CompilerParams(
            dimension_semantics=("arbitrary", "arbitrary"), vmem_limit_bytes=VMEM_LIMIT),
        name="mix1",
    )(x2, gmix, w_in_bf, w_in_bf, w_in_bf, w_in_bf, w_in_bf, par, wgate)


def _mix2_kernel(x_ref, a_ref, yc_ref, hy_ref, wgc_ref, wgr_ref, wa_ref, wb_ref, wo_ref,
                 gffn_ref, wrh_ref, wrl_ref, br_ref,
                 h_ref, xq_ref, rt_ref, cnt_ref,
                 cnt_s):
    i = pl.program_id(0)
    kc = pl.program_id(1)
    nkc = pl.num_programs(1)
    tm = x_ref.shape[0]
    cw = wa_ref.shape[1]

    @pl.when(jnp.logical_and(i == 0, kc == 0))
    def _():
        cnt_s[...] = jnp.zeros_like(cnt_s)

    @pl.when(kc == 0)
    def _():
        h_ref[...] = x_ref[...]

    wgc = wgc_ref[...]
    wgr = wgr_ref[...]
    wa = wa_ref[...]
    wb = wb_ref[...]
    wo = wo_ref[0]
    chains = [slice(c * MIX2_CHAIN, (c + 1) * MIX2_CHAIN) for c in range(tm // MIX2_CHAIN)]
    pre = []
    for rs in chains:
        g_conv = _dot(a_ref[rs, :], wgc)
        g_rec = _dot(a_ref[rs, :], wgr)
        y_conv = _dot(yc_ref[rs, :], wa)
        y_rec = _dot(hy_ref[rs, :], wb)
        pre.append((g_conv, g_rec, y_conv, y_rec))
    for rs, (g_conv, g_rec, y_conv, y_rec) in zip(chains, pre):
        merged = jax.nn.sigmoid(g_conv) * y_conv + jax.nn.sigmoid(g_rec) * y_rec
        h_ref[rs, :] += _dot(merged.astype(BF16), wo)

    @pl.when(kc == nkc - 1)
    def _():
        xn = _rms(h_ref[...], gffn_ref[...])
        xh = xn.astype(BF16)
        xhf = xh.astype(F32)
        xl = (xn - xhf).astype(BF16)
        half = xn.shape[1] // 2
        lo = pltpu.bitcast(xhf[:, :half], jnp.uint32) >> 16
        hi = pltpu.bitcast(xhf[:, half:], jnp.uint32) & jnp.uint32(0xFFFF0000)
        words = lo | hi
        tile_rows = half // LANES
        for q in range(tile_rows):
            xq_ref[pl.ds(q, tm, stride=tile_rows), :] = words[:, q * LANES:(q + 1) * LANES]
        wrh = wrh_ref[...]
        logits = _dot(xh, wrh) + _dot(xh, wrl_ref[...]) + _dot(xl, wrh) + br_ref[...]
        lane = lax.broadcasted_iota(jnp.int32, (tm, LANES), 1)
        lane_f = lane.astype(F32)
        work = jnp.where(lane < N_EXPERTS, logits, NEG_BIG)
        vals, idxs, hots = [], [], []
        for _k in range(TOP_K):
            m = jnp.max(work, axis=-1, keepdims=True)
            idx = jnp.min(jnp.where(work == m, lane_f, float(LANES)), axis=-1, keepdims=True)
            hot = lane_f == idx
            work = jnp.where(hot, NEG_BIG, work)
            vals.append(m)
            idxs.append(idx)
            hots.append(hot)
        es = [jnp.exp(v - vals[0]) for v in vals]
        denom = es[0] + es[1] + es[2] + es[3]
        ws = [e / denom for e in es]
        hot_any = jnp.zeros((tm, LANES), F32)
        for hot in hots:
            hot_any = hot_any + hot.astype(F32)
        rr = lax.broadcasted_iota(jnp.int32, (tm, tm), 0)
        cc = lax.broadcasted_iota(jnp.int32, (tm, tm), 1)
        lower = jnp.where(rr > cc, 1.0, 0.0).astype(BF16)
        cum = _dot(lower, hot_any.astype(BF16)) + cnt_s[0:1, :]
        out = jnp.zeros((tm, LANES), F32)
        for k in range(TOP_K):
            rank = jnp.sum(jnp.where(hots[k], cum, 0.0), axis=-1, keepdims=True)
            out = jnp.where(lane == k, idxs[k], out)
            out = jnp.where(lane == TOP_K + k, ws[k], out)
            out = jnp.where(lane == 2 * TOP_K + k, rank, out)
        rt_ref[...] = out
        cnt_new = cnt_s[...] + jnp.sum(hot_any, axis=0, keepdims=True)
        cnt_s[...] = cnt_new
        cnt_ref[...] = cnt_new


def _mix2(x2, a, yc, hy, w_in_bf, wa, wb, wo, gffn, wrh, wrl, br):
    n, d = x2.shape
    nkc, cw, _ = wo.shape
    tm = TM_MIX
    grid = (n // tm, nkc)
    row = lambda i, k: (i, 0)
    const = lambda i, k: (0, 0)
    chunk = lambda i, k: (k, 0, 0)
    return pl.pallas_call(
        _mix2_kernel,
        grid=grid,
        in_specs=[
            pl.BlockSpec((tm, d), row),
            pl.BlockSpec((tm, d), row),
            pl.BlockSpec((tm, d), row),
            pl.BlockSpec((tm, d), row),
            pl.BlockSpec((d, cw), lambda i, k: (0, 5 * nkc + k)),
            pl.BlockSpec((d, cw), lambda i, k: (0, 6 * nkc + k)),
            pl.BlockSpec((d, cw), lambda i, k: (0, k)),
            pl.BlockSpec((d, cw), lambda i, k: (0, k)),
            pl.BlockSpec((1, cw, d), chunk),
            pl.BlockSpec((1, d), const),
            pl.BlockSpec((d, LANES), const),
            pl.BlockSpec((d, LANES), const),
            pl.BlockSpec((1, LANES), const),
        ],
        out_specs=[
            pl.BlockSpec((tm, d), row),
            pl.BlockSpec((tm * (d // 2 // LANES), LANES), row),
            pl.BlockSpec((tm, LANES), row),
            pl.BlockSpec((SUBLANES, LANES), const),
        ],
        out_shape=[
            jax.ShapeDtypeStruct((n, d), F32),
            jax.ShapeDtypeStruct((n * (d // 2 // LANES), LANES), jnp.uint32),
            jax.ShapeDtypeStruct((n, LANES), F32),
            jax.ShapeDtypeStruct((SUBLANES, LANES), F32),
        ],
        scratch_shapes=[pltpu.VMEM((SUBLANES, LANES), F32)],
        compiler_params=pltpu.CompilerParams(
            dimension_semantics=("arbitrary", "arbitrary"), vmem_limit_bytes=VMEM_LIMIT),
        name="mix2",
    )(x2, a, yc, hy, w_in_bf, w_in_bf, wa, wb, wo, gffn, wrh, wrl, br)


def _moe_kernel(te_ref, ns_ref,
                tok_ref, xq_hbm, wgu_hbm, wd_hbm, bgu_ref, bd_ref,
                o_ref, gbuf, xb, wg_s, wu_s, wd_s, gsem, wsem):
    i = pl.program_id(0)
    nt = pl.num_programs(0) - 1
    tr, d = o_ref.shape
    sb = tr // 2
    fc = wg_s.shape[2]
    f = wd_hbm.shape[1]
    nf = f // fc
    half = d // 2
    tile_rows = half // LANES
    per_chunk = tr // (nf // 2)

    i_n = jnp.minimum(i, nt - 1)
    ns_next = jnp.where(i < nt, ns_ref[i_n], 0)
    slot_n = i % 2
    issue_iters = ns_next * (sb // per_chunk)
    ic = jnp.maximum(i - 1, 0)
    ns_cur = jnp.where(i > 0, ns_ref[ic], 0)
    slot_c = ic % 2
    e_cur = te_ref[ic]

    def start_row(r):
        src = pl.multiple_of(tok_ref[0, 0, r] * tile_rows, tile_rows)
        dst = pl.multiple_of(r * tile_rows, tile_rows)
        pltpu.make_async_copy(xq_hbm.at[pl.ds(src, tile_rows)], gbuf.at[slot_n, pl.ds(dst, tile_rows)],
                              gsem.at[slot_n]).start()

    def weight_copies(e, j, slot):
        col = pl.multiple_of(j * fc, fc)
        ucol = pl.multiple_of(f + j * fc, fc)
        return (
            pltpu.make_async_copy(wgu_hbm.at[e, :, pl.ds(col, fc)], wg_s.at[slot], wsem.at[slot, 0]),
            pltpu.make_async_copy(wgu_hbm.at[e, :, pl.ds(ucol, fc)], wu_s.at[slot], wsem.at[slot, 1]),
            pltpu.make_async_copy(wd_hbm.at[e, pl.ds(col, fc), :], wd_s.at[slot], wsem.at[slot, 2]),
        )

    @pl.when(jnp.logical_and(i == 0, ns_next > 0))
    def _():
        for cp in weight_copies(te_ref[0], 0, 0):
            cp.start()

    @pl.when(ns_cur > 0)
    def _():
        rows = ns_cur * (sb * tile_rows)
        pltpu.make_async_copy(xq_hbm.at[pl.ds(0, rows)], gbuf.at[slot_c, pl.ds(0, rows)],
                              gsem.at[slot_c]).wait()

    for sub in range(2):
        lo_r, hi_r = sub * sb, (sub + 1) * sb

        @pl.when(ns_cur > sub)
        def _(lo_r=lo_r, hi_r=hi_r):
            for q in range(tile_rows):
                w = gbuf[slot_c, pl.ds(lo_r * tile_rows + q, sb, stride=tile_rows), :]
                cl = slice(q * LANES, (q + 1) * LANES)
                ch = slice(half + q * LANES, half + (q + 1) * LANES)
                xb[lo_r:hi_r, cl] = pltpu.bitcast(w << 16, F32).astype(BF16)
                xb[lo_r:hi_r, ch] = pltpu.bitcast(w & jnp.uint32(0xFFFF0000), F32).astype(BF16)
            o_ref[lo_r:hi_r, :] = jnp.broadcast_to(bd_ref[0], (sb, d))

        @pl.when(ns_cur <= sub)
        def _(lo_r=lo_r, hi_r=hi_r):
            o_ref[lo_r:hi_r, :] = jnp.zeros((sb, d), o_ref.dtype)

    def chunk_body(j, nrows, with_issue):
        slot = j % 2
        for cp in weight_copies(e_cur, j, slot):
            cp.wait()
        last = j == nf - 1
        j_n = jnp.where(last, 0, j + 1)
        e_n = jnp.where(last, te_ref[i_n], e_cur)

        @pl.when(jnp.logical_or(jnp.logical_not(last), ns_next > 0))
        def _():
            for cp in weight_copies(e_n, j_n, 1 - slot):
                cp.start()

        if with_issue:
            for r in range(per_chunk):
                start_row(j * per_chunk + r)
        wgu = jnp.concatenate([wg_s[slot], wu_s[slot]], axis=1).astype(BF16)
        wdn = wd_s[slot].astype(BF16)
        bg = bgu_ref[0, pl.ds(j, 1), :]
        bu = bgu_ref[0, pl.ds(nf + j, 1), :]
        chain = min(MOE_CHAIN, nrows)
        chains = [slice(c * chain, (c + 1) * chain) for c in range(nrows // chain)]
        gus = [_dot(xb[rs, :], wgu) for rs in chains]
        for rs, gu in zip(chains, gus):
            gate = jnp.minimum(gu[:, :fc] + bg, SWIGLU_LIMIT)
            up = jnp.clip(gu[:, fc:] + bu, -SWIGLU_LIMIT, SWIGLU_LIMIT)
            glu = gate * jax.nn.sigmoid(SWIGLU_ALPHA * gate)
            act = ((up + 1.0) * glu).astype(BF16)
            o_ref[rs, :] += _dot(act, wdn)

    for nsub in (1, 2):
        @pl.when(ns_cur == nsub)
        def _(nsub=nsub):
            def with_issue(j, carry):
                chunk_body(j, nsub * sb, True)
                return carry

            def without_issue(j, carry):
                chunk_body(j, nsub * sb, False)
                return carry

            lax.fori_loop(0, issue_iters, with_issue, 0)
            lax.fori_loop(issue_iters, nf, without_issue, 0)

    @pl.when(jnp.logical_and(ns_cur == 0, ns_next > 0))
    def _():
        def issue(g, carry):
            for r in range(SUBLANES):
                start_row(g * SUBLANES + r)
            return carry

        lax.fori_loop(0, ns_next * (sb // SUBLANES), issue, 0)


def _moe(tile_e, tile_ns, slot_tok3, xq, wgu, bgu, wd, bd3):
    d = wgu.shape[1]
    f = wd.shape[1]
    half = d // 2
    nt, _, tr = slot_tok3.shape
    nf = f // FC
    bgu3 = bgu.reshape(bgu.shape[0], 2 * nf, FC)

    def cur(i):
        return jnp.maximum(i - 1, 0)

    grid_spec = pltpu.PrefetchScalarGridSpec(
        num_scalar_prefetch=2,
        grid=(nt + 1,),
        in_specs=[
            pl.BlockSpec((1, 1, tr), lambda i, te, ns: (jnp.minimum(i, nt - 1), 0, 0),
                         memory_space=pltpu.SMEM),
            pl.BlockSpec(memory_space=pl.ANY),
            pl.BlockSpec(memory_space=pl.ANY),
            pl.BlockSpec(memory_space=pl.ANY),
            pl.BlockSpec((1, 2 * nf, FC), lambda i, te, ns: (te[cur(i)], 0, 0)),
            pl.BlockSpec((1, 1, d), lambda i, te, ns: (te[cur(i)], 0, 0)),
        ],
        out_specs=pl.BlockSpec((tr, d), lambda i, te, ns: (cur(i), 0)),
        scratch_shapes=[
            pltpu.VMEM((2, tr * (half // LANES), LANES), jnp.uint32),
            pltpu.VMEM((tr, d), BF16),
            pltpu.VMEM((2, d, FC), F32),
            pltpu.VMEM((2, d, FC), F32),
            pltpu.VMEM((2, FC, d), F32),
            pltpu.SemaphoreType.DMA((2,)),
            pltpu.SemaphoreType.DMA((2, 3)),
        ],
    )
    return pl.pallas_call(
        _moe_kernel,
        grid_spec=grid_spec,
        out_shape=jax.ShapeDtypeStruct((nt * tr, d), F32),
        compiler_params=pltpu.CompilerParams(
            dimension_semantics=("arbitrary",), vmem_limit_bytes=VMEM_LIMIT),
        name="moe",
    )(tile_e, tile_ns, slot_tok3, xq, wgu, wd, bgu3, bd3)


def _combine_kernel(dst_ref, h_ref, ys_hbm, rt_ref, p_ref, gple_ref, wpg_ref, wpp_ref, gfin_ref,
                    o_ref, ybuf_a, ybuf_b, sem):
    i = pl.program_id(0)
    nt = pl.num_programs(0) - 1
    tm = h_ref.shape[0]
    bufs = (ybuf_a, ybuf_b)

    def start_token(r, slot):
        for k in range(TOP_K):
            s = dst_ref[0, 0, r * TOP_K + k]
            pltpu.make_async_copy(ys_hbm.at[pl.ds(s, 1)], bufs[slot].at[k, pl.ds(r, 1)],
                                  sem.at[slot]).start(priority=k % 2)

    @pl.when(i == 0)
    def _():
        def issue(r, carry):
            start_token(r, 0)
            return carry

        lax.fori_loop(0, tm, issue, 0, unroll=4)

    def compute(slot, with_issue):
        ybuf = bufs[slot]
        for k in range(TOP_K):
            pltpu.make_async_copy(ys_hbm.at[pl.ds(0, tm)], ybuf.at[k], sem.at[slot]).wait()
        if with_issue:
            for r in range(tm):
                start_token(r, 1 - slot)
        rt = rt_ref[...]
        y = rt[:, TOP_K:TOP_K + 1] * ybuf[0]
        for k in range(1, TOP_K):
            y = y + rt[:, TOP_K + k:TOP_K + k + 1] * ybuf[k]
        h2 = h_ref[...] + y
        gate = jax.nn.sigmoid(_dot(_rms(h2, gple_ref[...]).astype(BF16), wpg_ref[...]))
        pp = _dot(p_ref[...].astype(BF16), wpp_ref[...])
        h3 = h2 + gate * pp
        o_ref[...] = _rms(h3, gfin_ref[...])

    for slot in range(2):
        @pl.when(jnp.logical_and(jnp.logical_and(i > 0, i < nt), (i - 1) % 2 == slot))
        def _(slot=slot):
            compute(slot, True)

        @pl.when(jnp.logical_and(i == nt, (i - 1) % 2 == slot))
        def _(slot=slot):
            compute(slot, False)


def _combine(dest3, h, ys, rt, p2, gple, wpg, wpp, gfin):
    n, d = h.shape
    tm = TM_OUT
    nt = n // tm
    pd = p2.shape[1]
    cur = lambda i: (jnp.maximum(i - 1, 0), 0)
    const = lambda i: (0, 0)
    return pl.pallas_call(
        _combine_kernel,
        grid=(nt + 1,),
        in_specs=[
            pl.BlockSpec((1, 1, tm * TOP_K), lambda i: (jnp.minimum(i, nt - 1), 0, 0),
                         memory_space=pltpu.SMEM),
            pl.BlockSpec((tm, d), cur),
            pl.BlockSpec(memory_space=pl.ANY),
            pl.BlockSpec((tm, LANES), cur),
            pl.BlockSpec((tm, pd), cur),
            pl.BlockSpec((1, d), const),
            pl.BlockSpec((d, d), const),
            pl.BlockSpec((pd, d), const),
            pl.BlockSpec((1, d), const),
        ],
        out_specs=pl.BlockSpec((tm, d), cur),
        out_shape=jax.ShapeDtypeStruct((n, d), F32),
        scratch_shapes=[
            pltpu.VMEM((TOP_K, tm, d), F32),
            pltpu.VMEM((TOP_K, tm, d), F32),
            pltpu.SemaphoreType.DMA((2,)),
        ],
        compiler_params=pltpu.CompilerParams(
            dimension_semantics=("arbitrary",), vmem_limit_bytes=VMEM_LIMIT),
        name="combine",
    )(dest3, h, ys, rt, p2, gple, wpg, wpp, gfin)


def _layer(h2, p2, norm_mix, w_in, conv_a_w, w_a_out, conv_b_w, conv_b_b, w_rg, b_rg, w_ig, b_ig,
           lru_lambda, w_b_out, w_o, norm_ffn, w_router, b_router, w_gu, b_gu, w_down, b_down,
           norm_ple, w_ple_gate, w_ple_proj, norm_out):
    n, d = h2.shape
    nch = LRU_HEADS
    cw = d // nch

    w_in_bf = w_in.astype(BF16)
    wa = w_a_out.astype(BF16)
    wb = w_b_out.astype(BF16)
    wo = w_o.reshape(nch, cw, d).astype(BF16)
    wgate = jnp.concatenate([w_rg, w_ig], axis=-1).astype(BF16)
    par = jnp.concatenate([conv_a_w, conv_b_w, conv_b_b[None], b_rg[None], b_ig[None],
                           lru_lambda[None], jnp.zeros((16 - 11, d), F32)], axis=0)
    wr_pad = jnp.pad(w_router, ((0, 0), (0, LANES - N_EXPERTS)))
    wrh = wr_pad.astype(BF16)
    wrl = (wr_pad - wrh.astype(F32)).astype(BF16)
    br = jnp.pad(b_router, (0, LANES - N_EXPERTS))[None]

    a, yc, hy = _mix1(h2, norm_mix[None], w_in_bf, par, wgate)
    h_mid, xq, rt, cnt = _mix2(h2, a, yc, hy, w_in_bf, wa, wb, wo, norm_ffn[None], wrh, wrl, br)

    top_idx = rt[:, 0:TOP_K].astype(jnp.int32)
    rank = rt[:, 2 * TOP_K:3 * TOP_K].astype(jnp.int32)
    counts = cnt[0, :N_EXPERTS].astype(jnp.int32)
    sb = TR // 2
    nt = -(-(n * TOP_K) // TR) + N_EXPERTS
    n_sub = (counts + sb - 1) // sb
    n_tile = (n_sub + 1) // 2
    t_ends = jnp.cumsum(n_tile)
    t_starts = t_ends - n_tile
    n_tiles = t_ends[-1]
    dest = (t_starts * TR)[top_idx] + rank
    tok = jnp.broadcast_to(jnp.arange(n, dtype=jnp.int32)[:, None], (n, TOP_K))
    slot_tok = jnp.zeros((nt * TR,), jnp.int32).at[dest.reshape(-1)].set(tok.reshape(-1))
    ti = jnp.arange(nt, dtype=jnp.int32)
    e_of = jnp.minimum(jnp.sum((t_ends[None, :] <= ti[:, None]).astype(jnp.int32), axis=1), N_EXPERTS - 1)
    tile_ns = jnp.where(ti < n_tiles, jnp.clip(n_sub[e_of] - 2 * (ti - t_starts[e_of]), 0, 2), 0)
    tile_e = jnp.where(ti < n_tiles, e_of, e_of[n_tiles - 1])

    ys = _moe(tile_e.astype(jnp.int32), tile_ns.astype(jnp.int32), slot_tok.reshape(nt, 1, TR), xq,
              w_gu, b_gu, w_down, b_down[:, None, :])

    out = _combine(dest.reshape(n // TM_OUT, 1, TM_OUT * TOP_K), h_mid, ys, rt, p2, norm_ple[None],
                   w_ple_gate.astype(BF16), w_ple_proj.astype(BF16), norm_out[None])
    return out


def kernel(x, p, norm_mix, w_in, conv_a_w, w_a_out, conv_b_w, conv_b_b, w_rg, b_rg, w_ig, b_ig, lru_lambda, w_b_out, w_o, norm_ffn, w_router, b_router, w_gu, b_gu, w_down, b_down, norm_ple, w_ple_gate, w_ple_proj, norm_final):
    bsz, t, d = x.shape
    depth = p.shape[0]
    assert bsz == 1 and depth == 1, "kernel fuses the final norm into the single layer"
    out = _layer(x.reshape(t, d), p[0].reshape(t, -1), norm_mix[0], w_in[0], conv_a_w[0], w_a_out[0],
                 conv_b_w[0], conv_b_b[0], w_rg[0], b_rg[0], w_ig[0], b_ig[0], lru_lambda[0], w_b_out[0],
                 w_o[0], norm_ffn[0], w_router[0], b_router[0], w_gu[0], b_gu[0], w_down[0], b_down[0],
                 norm_ple[0], w_ple_gate[0], w_ple_proj[0], norm_final)
    return out.reshape(bsz, t, d)
```

```python
import functools

import jax
import jax.numpy as jnp
from jax import lax
from jax.experimental import pallas as pl
from jax.experimental.pallas import tpu as pltpu

F32 = jnp.float32
BF16 = jnp.bfloat16

NORM_EPS = 1e-6
LRU_HEADS = 8
LRU_C = 8.0
N_EXPERTS = 32
TOP_K = 4
SWIGLU_LIMIT = 7.0
SWIGLU_ALPHA = 1.702

LANES = 128
SUBLANES = 8
VMEM_LIMIT = 56 * 1024 * 1024

TM_MIX = 512
TM_OUT = 256
TR = 1024
FC = 256
MIX1_CPS = 2
MOE_CHAIN = 128
MIX2_CHAIN = 256
NEG_BIG = -3.0e38


def _rms(x, g):
    ms = jnp.mean(x * x, axis=-1, keepdims=True)
    return x * lax.rsqrt(ms + NORM_EPS) * g


def _dot(a, b):
    return jnp.dot(a, b, preferred_element_type=F32)


def _shift_rows(u, hist8, k):
    rolled = pltpu.roll(u, k, axis=0)
    hr = pltpu.roll(hist8, k, axis=0)
    row = lax.broadcasted_iota(jnp.int32, hist8.shape, 0)
    first = jnp.where(row < k, hr, rolled[:SUBLANES])
    return jnp.concatenate([first, rolled[SUBLANES:]], axis=0)


def _gelu_tanh(x):
    return 0.5 * x * (1.0 + jnp.tanh(0.7978845608028654 * (x + 0.044715 * (x * x * x))))


def _mix1_kernel(x_ref, gmix_ref, wpb_ref, wpc_ref, wpv_ref, wpy_ref, wpx_ref, par_ref, wgate_ref,
                 a_ref, yc_ref, hy_ref,
                 hu_s, hx_s, hp_s, h_s):
    i = pl.program_id(0)
    c = pl.program_id(1)
    tm = yc_ref.shape[0]
    cps = wgate_ref.shape[0]
    cw = yc_ref.shape[1] // cps

    @pl.when(jnp.logical_and(i == 0, c == 0))
    def _():
        hu_s[...] = jnp.zeros_like(hu_s)
        hx_s[...] = jnp.zeros_like(hx_s)
        hp_s[...] = jnp.zeros_like(hp_s)

    @pl.when(c == 0)
    def _():
        a_ref[...] = _rms(x_ref[...], gmix_ref[...]).astype(BF16)

    trow = lax.broadcasted_iota(jnp.int32, (tm, cw), 0)
    sub = trow % SUBLANES
    first_step = trow + i * tm == 0

    for q in range(cps):
        ch = c * cps + q
        cols = slice(q * cw, (q + 1) * cw)
        a_bf = a_ref[...]
        x_r = _dot(a_bf, wpx_ref[:, cols])
        b_c = _dot(a_bf, wpb_ref[:, cols])
        c_c = _dot(a_bf, wpc_ref[:, cols])
        v_c = _dot(a_bf, wpv_ref[:, cols])
        y_r = _dot(a_bf, wpy_ref[:, cols])
        par = par_ref[:, cols]

        hx = hx_s[ch]
        xc = (par[3:4] * _shift_rows(x_r, hx, 3) + par[4:5] * _shift_rows(x_r, hx, 2)
              + par[5:6] * _shift_rows(x_r, hx, 1) + par[6:7] * x_r) + par[7:8]
        hx_s[ch] = x_r[tm - SUBLANES:, :]
        gz = _dot(xc.astype(BF16), wgate_ref[q])

        u = c_c * v_c
        hu = hu_s[ch]
        conv = (par[0:1] * _shift_rows(u, hu, 2) + par[1:2] * _shift_rows(u, hu, 1) + par[2:3] * u)
        yc_ref[:, cols] = (b_c * conv).astype(BF16)
        hu_s[ch] = u[tm - SUBLANES:, :]

        r = jax.nn.sigmoid(gz[:, :cw] + par[8:9])
        ig = jax.nn.sigmoid(gz[:, cw:] + par[9:10])
        nlam = -par[10:11]
        softplus = jnp.maximum(nlam, 0.0) + jnp.log1p(jnp.exp(-jnp.abs(nlam)))
        log_a = (-LRU_C) * r * softplus
        a = jnp.exp(log_a)
        mult = jnp.sqrt(1.0 - a * a)
        mult = jnp.where(first_step, 1.0, mult)
        b = xc * ig * mult

        for s in (1, 2, 4):
            a_sh = pltpu.roll(a, s, axis=0)
            b_sh = pltpu.roll(b, s, axis=0)
            m = sub >= s
            b = jnp.where(m, a * b_sh + b, b)
            a = jnp.where(m, a * a_sh, a)
        carry = hp_s[ch][SUBLANES - 1:SUBLANES, :]
        for g in range(tm // SUBLANES):
            lo = g * SUBLANES
            hg = b[lo:lo + SUBLANES, :] + a[lo:lo + SUBLANES, :] * carry
            h_s[q, lo:lo + SUBLANES, :] = hg
            carry = hg[SUBLANES - 1:SUBLANES, :]
        hp_s[ch] = h_s[q, tm - SUBLANES:, :]
        hy_ref[:, cols] = (h_s[q] * _gelu_tanh(y_r)).astype(BF16)


def _mix1(x2, gmix, w_in_bf, par, wgate):
    n, d = x2.shape
    nch, cw, _ = wgate.shape
    tm = TM_MIX
    cps = MIX1_CPS
    steps = nch // cps
    grid = (n // tm, steps)
    split = lambda s: pl.BlockSpec((d, cps * cw), lambda i, c: (0, s * steps + c))
    return pl.pallas_call(
        _mix1_kernel,
        grid=grid,
        in_specs=[
            pl.BlockSpec((tm, d), lambda i, c: (i, 0)),
            pl.BlockSpec((1, d), lambda i, c: (0, 0)),
            split(0), split(1), split(2), split(3), split(4),
            pl.BlockSpec((16, cps * cw), lambda i, c: (0, c)),
            pl.BlockSpec((cps, cw, 2 * cw), lambda i, c: (c, 0, 0)),
        ],
        out_specs=[
            pl.BlockSpec((tm, d), lambda i, c: (i, 0)),
            pl.BlockSpec((tm, cps * cw), lambda i, c: (i, c)),
            pl.BlockSpec((tm, cps * cw), lambda i, c: (i, c)),
        ],
        out_shape=[
            jax.ShapeDtypeStruct((n, d), BF16),
            jax.ShapeDtypeStruct((n, d), BF16),
            jax.ShapeDtypeStruct((n, d), BF16),
        ],
        scratch_shapes=[
            pltpu.VMEM((nch, SUBLANES, cw), F32),
            pltpu.VMEM((nch, SUBLANES, cw), F32),
            pltpu.VMEM((nch, SUBLANES, cw), F32),
            pltpu.VMEM((cps, tm, cw), F32),
        ],
        compiler_params=pltpu.CompilerParams(
            dimension_semantics=("arbitrary", "arbitrary"), vmem_limit_bytes=VMEM_LIMIT),
        name="mix1",
    )(x2, gmix, w_in_bf, w_in_bf, w_in_bf, w_in_bf, w_in_bf, par, wgate)


def _mix2_kernel(x_ref, a_ref, yc_ref, hy_ref, wgc_ref, wgr_ref, wa_ref, wb_ref, wo_ref,
                 gffn_ref, wrh_ref, wrl_ref, br_ref,
                 h_ref, xq_ref, rt_ref, cnt_ref,
                 cnt_s):
    i = pl.program_id(0)
    kc = pl.program_id(1)
    nkc = pl.num_programs(1)
    tm = x_ref.shape[0]
    cw = wa_ref.shape[1]

    @pl.when(jnp.logical_and(i == 0, kc == 0))
    def _():
        cnt_s[...] = jnp.zeros_like(cnt_s)

    @pl.when(kc == 0)
    def _():
        h_ref[...] = x_ref[...]

    wgc = wgc_ref[...]
    wgr = wgr_ref[...]
    wa = wa_ref[...]
    wb = wb_ref[...]
    wo = wo_ref[0]
    chains = [slice(c * MIX2_CHAIN, (c + 1) * MIX2_CHAIN) for c in range(tm // MIX2_CHAIN)]
    pre = []
    for rs in chains:
        g_conv = _dot(a_ref[rs, :], wgc)
        g_rec = _dot(a_ref[rs, :], wgr)
        y_conv = _dot(yc_ref[rs, :], wa)
        y_rec = _dot(hy_ref[rs, :], wb)
        pre.append((g_conv, g_rec, y_conv, y_rec))
    for rs, (g_conv, g_rec, y_conv, y_rec) in zip(chains, pre):
        merged = jax.nn.sigmoid(g_conv) * y_conv + jax.nn.sigmoid(g_rec) * y_rec
        h_ref[rs, :] += _dot(merged.astype(BF16), wo)

    @pl.when(kc == nkc - 1)
    def _():
        xn = _rms(h_ref[...], gffn_ref[...])
        xh = xn.astype(BF16)
        xhf = xh.astype(F32)
        xl = (xn - xhf).astype(BF16)
        half = xn.shape[1] // 2
        lo = pltpu.bitcast(xhf[:, :half], jnp.uint32) >> 16
        hi = pltpu.bitcast(xhf[:, half:], jnp.uint32) & jnp.uint32(0xFFFF0000)
        words = lo | hi
        tile_rows = half // LANES
        for q in range(tile_rows):
            xq_ref[pl.ds(q, tm, stride=tile_rows), :] = words[:, q * LANES:(q + 1) * LANES]
        wrh = wrh_ref[...]
        logits = _dot(xh, wrh) + _dot(xh, wrl_ref[...]) + _dot(xl, wrh) + br_ref[...]
        lane = lax.broadcasted_iota(jnp.int32, (tm, LANES), 1)
        lane_f = lane.astype(F32)
        work = jnp.where(lane < N_EXPERTS, logits, NEG_BIG)
        vals, idxs, hots = [], [], []
        for _k in range(TOP_K):
            m = jnp.max(work, axis=-1, keepdims=True)
            idx = jnp.min(jnp.where(work == m, lane_f, float(LANES)), axis=-1, keepdims=True)
            hot = lane_f == idx
            work = jnp.where(hot, NEG_BIG, work)
            vals.append(m)
            idxs.append(idx)
            hots.append(hot)
        es = [jnp.exp(v - vals[0]) for v in vals]
        denom = es[0] + es[1] + es[2] + es[3]
        ws = [e / denom for e in es]
        hot_any = jnp.zeros((tm, LANES), F32)
        for hot in hots:
            hot_any = hot_any + hot.astype(F32)
        rr = lax.broadcasted_iota(jnp.int32, (tm, tm), 0)
        cc = lax.broadcasted_iota(jnp.int32, (tm, tm), 1)
        lower = jnp.where(rr > cc, 1.0, 0.0).astype(BF16)
        cum = _dot(lower, hot_any.astype(BF16)) + cnt_s[0:1, :]
        out = jnp.zeros((tm, LANES), F32)
        for k in range(TOP_K):
            rank = jnp.sum(jnp.where(hots[k], cum, 0.0), axis=-1, keepdims=True)
            out = jnp.where(lane == k, idxs[k], out)
            out = jnp.where(lane == TOP_K + k, ws[k], out)
            out = jnp.where(lane == 2 * TOP_K + k, rank, out)
        rt_ref[...] = out
        cnt_new = cnt_s[...] + jnp.sum(hot_any, axis=0, keepdims=True)
        cnt_s[...] = cnt_new
        cnt_ref[...] = cnt_new


def _mix2(x2, a, yc, hy, w_in_bf, wa, wb, wo, gffn, wrh, wrl, br):
    n, d = x2.shape
    nkc, cw, _ = wo.shape
    tm = TM_MIX
    grid = (n // tm, nkc)
    row = lambda i, k: (i, 0)
    const = lambda i, k: (0, 0)
    chunk = lambda i, k: (k, 0, 0)
    return pl.pallas_call(
        _mix2_kernel,
        grid=grid,
        in_specs=[
            pl.BlockSpec((tm, d), row),
            pl.BlockSpec((tm, d), row),
            pl.BlockSpec((tm, d), row),
            pl.BlockSpec((tm, d), row),
            pl.BlockSpec((d, cw), lambda i, k: (0, 5 * nkc + k)),
            pl.BlockSpec((d, cw), lambda i, k: (0, 6 * nkc + k)),
            pl.BlockSpec((d, cw), lambda i, k: (0, k)),
            pl.BlockSpec((d, cw), lambda i, k: (0, k)),
            pl.BlockSpec((1, cw, d), chunk),
            pl.BlockSpec((1, d), const),
            pl.BlockSpec((d, LANES), const),
            pl.BlockSpec((d, LANES), const),
            pl.BlockSpec((1, LANES), const),
        ],
        out_specs=[
            pl.BlockSpec((tm, d), row),
            pl.BlockSpec((tm * (d // 2 // LANES), LANES), row),
            pl.BlockSpec((tm, LANES), row),
            pl.BlockSpec((SUBLANES, LANES), const),
        ],
        out_shape=[
            jax.ShapeDtypeStruct((n, d), F32),
            jax.ShapeDtypeStruct((n * (d // 2 // LANES), LANES), jnp.uint32),
            jax.ShapeDtypeStruct((n, LANES), F32),
            jax.ShapeDtypeStruct((SUBLANES, LANES), F32),
        ],
        scratch_shapes=[pltpu.VMEM((SUBLANES, LANES), F32)],
        compiler_params=pltpu.CompilerParams(
            dimension_semantics=("arbitrary", "arbitrary"), vmem_limit_bytes=VMEM_LIMIT),
        name="mix2",
    )(x2, a, yc, hy, w_in_bf, w_in_bf, wa, wb, wo, gffn, wrh, wrl, br)


def _moe_kernel(te_ref, ns_ref,
                tok_ref, xq_hbm, wgu_hbm, wd_hbm, bgu_ref, bd_ref,
                o_ref, gbuf, xb, wg_s, wu_s, wd_s, gsem, wsem):
    i = pl.program_id(0)
    nt = pl.num_programs(0) - 1
    tr, d = o_ref.shape
    sb = tr // 2
    fc = wg_s.shape[2]
    f = wd_hbm.shape[1]
    nf = f // fc
    half = d // 2
    tile_rows = half // LANES
    per_chunk = tr // (nf // 2)

    i_n = jnp.minimum(i, nt - 1)
    ns_next = jnp.where(i < nt, ns_ref[i_n], 0)
    slot_n = i % 2
    issue_iters = ns_next * (sb // per_chunk)
    ic = jnp.maximum(i - 1, 0)
    ns_cur = jnp.where(i > 0, ns_ref[ic], 0)
    slot_c = ic % 2
    e_cur = te_ref[ic]

    def start_row(r):
        src = pl.multiple_of(tok_ref[0, 0, r] * tile_rows, tile_rows)
        dst = pl.multiple_of(r * tile_rows, tile_rows)
        pltpu.make_async_copy(xq_hbm.at[pl.ds(src, tile_rows)], gbuf.at[slot_n, pl.ds(dst, tile_rows)],
                              gsem.at[slot_n]).start()

    def weight_copies(e, j, slot):
        aligned = lambda v: v if isinstance(v, int) else pl.multiple_of(v, fc)
        col = aligned(j * fc)
        ucol = aligned(f + j * fc)
        return (
            pltpu.make_async_copy(wgu_hbm.at[e, :, pl.ds(col, fc)], wg_s.at[slot], wsem.at[slot, 0]),
            pltpu.make_async_copy(wgu_hbm.at[e, :, pl.ds(ucol, fc)], wu_s.at[slot], wsem.at[slot, 1]),
            pltpu.make_async_copy(wd_hbm.at[e, pl.ds(col, fc), :], wd_s.at[slot], wsem.at[slot, 2]),
        )

    @pl.when(jnp.logical_and(i == 0, ns_next > 0))
    def _():
        for cp in weight_copies(te_ref[0], 0, 0):
            cp.start(priority=1)

    @pl.when(ns_cur > 0)
    def _():
        rows = ns_cur * (sb * tile_rows)
        pltpu.make_async_copy(xq_hbm.at[pl.ds(0, rows)], gbuf.at[slot_c, pl.ds(0, rows)],
                              gsem.at[slot_c]).wait()

    for sub in range(2):
        lo_r, hi_r = sub * sb, (sub + 1) * sb

        @pl.when(ns_cur > sub)
        def _(lo_r=lo_r, hi_r=hi_r):
            for q in range(tile_rows):
                w = gbuf[slot_c, pl.ds(lo_r * tile_rows + q, sb, stride=tile_rows), :]
                cl = slice(q * LANES, (q + 1) * LANES)
                ch = slice(half + q * LANES, half + (q + 1) * LANES)
                xb[lo_r:hi_r, cl] = pltpu.bitcast(w << 16, F32).astype(BF16)
                xb[lo_r:hi_r, ch] = pltpu.bitcast(w & jnp.uint32(0xFFFF0000), F32).astype(BF16)

        @pl.when(ns_cur <= sub)
        def _(lo_r=lo_r, hi_r=hi_r):
            o_ref[lo_r:hi_r, :] = jnp.zeros((sb, d), o_ref.dtype)

    def chunk_body(j, nrows, with_issue, first=False):
        slot = j % 2
        for cp in weight_copies(e_cur, j, slot):
            cp.wait()
        last = j == nf - 1
        j_n = jnp.where(last, 0, j + 1)
        e_n = jnp.where(last, te_ref[i_n], e_cur)

        @pl.when(jnp.logical_or(jnp.logical_not(last), ns_next > 0))
        def _():
            for cp in weight_copies(e_n, j_n, 1 - slot):
                cp.start(priority=1)

        if with_issue:
            for r in range(per_chunk):
                start_row(j * per_chunk + r)
        wgu = jnp.concatenate([wg_s[slot], wu_s[slot]], axis=1).astype(BF16)
        wdn = wd_s[slot].astype(BF16)
        bg = bgu_ref[0, pl.ds(j, 1), :]
        bu = bgu_ref[0, pl.ds(nf + j, 1), :]
        chain = min(MOE_CHAIN, nrows)
        chains = [slice(c * chain, (c + 1) * chain) for c in range(nrows // chain)]
        gus = [_dot(xb[rs, :], wgu) for rs in chains]
        for rs, gu in zip(chains, gus):
            gate = jnp.minimum(gu[:, :fc] + bg, SWIGLU_LIMIT)
            up = jnp.clip(gu[:, fc:] + bu, -SWIGLU_LIMIT, SWIGLU_LIMIT)
            glu = gate * jax.nn.sigmoid(SWIGLU_ALPHA * gate)
            act = ((up + 1.0) * glu).astype(BF16)
            if first:
                o_ref[rs, :] = _dot(act, wdn) + bd_ref[0]
            else:
                o_ref[rs, :] += _dot(act, wdn)

    def run_tile(nrows, issuing):
        def with_issue(j, carry):
            chunk_body(j, nrows, True)
            return carry

        def without_issue(j, carry):
            chunk_body(j, nrows, False)
            return carry

        chunk_body(0, nrows, issuing, first=True)
        if issuing:
            lax.fori_loop(1, issue_iters, with_issue, 0)
            lax.fori_loop(issue_iters, nf, without_issue, 0)
        else:
            lax.fori_loop(1, nf, without_issue, 0)

    for nsub in (1, 2):
        for issuing in (True, False):
            @pl.when(jnp.logical_and(ns_cur == nsub, (issue_iters > 0) == issuing))
            def _(nsub=nsub, issuing=issuing):
                run_tile(nsub * sb, issuing)

    @pl.when(jnp.logical_and(ns_cur == 0, ns_next > 0))
    def _():
        def issue(g, carry):
            for r in range(SUBLANES):
                start_row(g * SUBLANES + r)
            return carry

        lax.fori_loop(0, ns_next * (sb // SUBLANES), issue, 0)


def _moe(tile_e, tile_ns, slot_tok3, xq, wgu, bgu, wd, bd3):
    d = wgu.shape[1]
    f = wd.shape[1]
    half = d // 2
    nt, _, tr = slot_tok3.shape
    nf = f // FC
    bgu3 = bgu.reshape(bgu.shape[0], 2 * nf, FC)

    def cur(i):
        return jnp.maximum(i - 1, 0)

    grid_spec = pltpu.PrefetchScalarGridSpec(
        num_scalar_prefetch=2,
        grid=(nt + 1,),
        in_specs=[
            pl.BlockSpec((1, 1, tr), lambda i, te, ns: (jnp.minimum(i, nt - 1), 0, 0),
                         memory_space=pltpu.SMEM),
            pl.BlockSpec(memory_space=pl.ANY),
            pl.BlockSpec(memory_space=pl.ANY),
            pl.BlockSpec(memory_space=pl.ANY),
            pl.BlockSpec((1, 2 * nf, FC), lambda i, te, ns: (te[cur(i)], 0, 0)),
            pl.BlockSpec((1, 1, d), lambda i, te, ns: (te[cur(i)], 0, 0)),
        ],
        out_specs=pl.BlockSpec((tr, d), lambda i, te, ns: (cur(i), 0)),
        scratch_shapes=[
            pltpu.VMEM((2, tr * (half // LANES), LANES), jnp.uint32),
            pltpu.VMEM((tr, d), BF16),
            pltpu.VMEM((2, d, FC), F32),
            pltpu.VMEM((2, d, FC), F32),
            pltpu.VMEM((2, FC, d), F32),
            pltpu.SemaphoreType.DMA((2,)),
            pltpu.SemaphoreType.DMA((2, 3)),
        ],
    )
    return pl.pallas_call(
        _moe_kernel,
        grid_spec=grid_spec,
        out_shape=jax.ShapeDtypeStruct((nt * tr, d), F32),
        compiler_params=pltpu.CompilerParams(
            dimension_semantics=("arbitrary",), vmem_limit_bytes=VMEM_LIMIT),
        name="moe",
    )(tile_e, tile_ns, slot_tok3, xq, wgu, wd, bgu3, bd3)


def _combine_kernel(dst_ref, h_ref, ys_hbm, rt_ref, p_ref, gple_ref, wpg_ref, wpp_ref, gfin_ref,
                    o_ref, ybuf_a, ybuf_b, sem):
    i = pl.program_id(0)
    nt = pl.num_programs(0) - 1
    tm = h_ref.shape[0]
    bufs = (ybuf_a, ybuf_b)

    def start_token(r, slot):
        for k in range(TOP_K):
            s = dst_ref[0, 0, r * TOP_K + k]
            pltpu.make_async_copy(ys_hbm.at[pl.ds(s, 1)], bufs[slot].at[k, pl.ds(r, 1)],
                                  sem.at[slot]).start(priority=k % 2)

    @pl.when(i == 0)
    def _():
        def issue(r, carry):
            start_token(r, 0)
            return carry

        lax.fori_loop(0, tm, issue, 0, unroll=4)

    def compute(slot, with_issue):
        ybuf = bufs[slot]
        for k in range(TOP_K):
            pltpu.make_async_copy(ys_hbm.at[pl.ds(0, tm)], ybuf.at[k], sem.at[slot]).wait()
        if with_issue:
            for r in range(tm):
                start_token(r, 1 - slot)
        rt = rt_ref[...]
        y = rt[:, TOP_K:TOP_K + 1] * ybuf[0]
        for k in range(1, TOP_K):
            y = y + rt[:, TOP_K + k:TOP_K + k + 1] * ybuf[k]
        h2 = h_ref[...] + y
        gate = jax.nn.sigmoid(_dot(_rms(h2, gple_ref[...]).astype(BF16), wpg_ref[...]))
        pp = _dot(p_ref[...].astype(BF16), wpp_ref[...])
        h3 = h2 + gate * pp
        o_ref[...] = _rms(h3, gfin_ref[...])

    for slot in range(2):
        @pl.when(jnp.logical_and(jnp.logical_and(i > 0, i < nt), (i - 1) % 2 == slot))
        def _(slot=slot):
            compute(slot, True)

        @pl.when(jnp.logical_and(i == nt, (i - 1) % 2 == slot))
        def _(slot=slot):
            compute(slot, False)


def _combine(dest3, h, ys, rt, p2, gple, wpg, wpp, gfin):
    n, d = h.shape
    tm = TM_OUT
    nt = n // tm
    pd = p2.shape[1]
    cur = lambda i: (jnp.maximum(i - 1, 0), 0)
    const = lambda i: (0, 0)
    return pl.pallas_call(
        _combine_kernel,
        grid=(nt + 1,),
        in_specs=[
            pl.BlockSpec((1, 1, tm * TOP_K), lambda i: (jnp.minimum(i, nt - 1), 0, 0),
                         memory_space=pltpu.SMEM),
            pl.BlockSpec((tm, d), cur),
            pl.BlockSpec(memory_space=pl.ANY),
            pl.BlockSpec((tm, LANES), cur),
            pl.BlockSpec((tm, pd), cur),
            pl.BlockSpec((1, d), const),
            pl.BlockSpec((d, d), const),
            pl.BlockSpec((pd, d), const),
            pl.BlockSpec((1, d), const),
        ],
        out_specs=pl.BlockSpec((tm, d), cur),
        out_shape=jax.ShapeDtypeStruct((n, d), F32),
        scratch_shapes=[
            pltpu.VMEM((TOP_K, tm, d), F32),
            pltpu.VMEM((TOP_K, tm, d), F32),
            pltpu.SemaphoreType.DMA((2,)),
        ],
        compiler_params=pltpu.CompilerParams(
            dimension_semantics=("arbitrary",), vmem_limit_bytes=VMEM_LIMIT),
        name="combine",
    )(dest3, h, ys, rt, p2, gple, wpg, wpp, gfin)


def _layer(h2, p2, norm_mix, w_in, conv_a_w, w_a_out, conv_b_w, conv_b_b, w_rg, b_rg, w_ig, b_ig,
           lru_lambda, w_b_out, w_o, norm_ffn, w_router, b_router, w_gu, b_gu, w_down, b_down,
           norm_ple, w_ple_gate, w_ple_proj, norm_out):
    n, d = h2.shape
    nch = LRU_HEADS
    cw = d // nch

    w_in_bf = w_in.astype(BF16)
    wa = w_a_out.astype(BF16)
    wb = w_b_out.astype(BF16)
    wo = w_o.reshape(nch, cw, d).astype(BF16)
    wgate = jnp.concatenate([w_rg, w_ig], axis=-1).astype(BF16)
    par = jnp.concatenate([conv_a_w, conv_b_w, conv_b_b[None], b_rg[None], b_ig[None],
                           lru_lambda[None], jnp.zeros((16 - 11, d), F32)], axis=0)
    wr_pad = jnp.pad(w_router, ((0, 0), (0, LANES - N_EXPERTS)))
    wrh = wr_pad.astype(BF16)
    wrl = (wr_pad - wrh.astype(F32)).astype(BF16)
    br = jnp.pad(b_router, (0, LANES - N_EXPERTS))[None]

    a, yc, hy = _mix1(h2, norm_mix[None], w_in_bf, par, wgate)
    h_mid, xq, rt, cnt = _mix2(h2, a, yc, hy, w_in_bf, wa, wb, wo, norm_ffn[None], wrh, wrl, br)

    top_idx = rt[:, 0:TOP_K].astype(jnp.int32)
    rank = rt[:, 2 * TOP_K:3 * TOP_K].astype(jnp.int32)
    counts = cnt[0, :N_EXPERTS].astype(jnp.int32)
    sb = TR // 2
    nt = -(-(n * TOP_K) // TR) + N_EXPERTS
    n_sub = (counts + sb - 1) // sb
    n_tile = (n_sub + 1) // 2
    t_ends = jnp.cumsum(n_tile)
    t_starts = t_ends - n_tile
    n_tiles = t_ends[-1]
    dest = (t_starts * TR)[top_idx] + rank
    tok = jnp.broadcast_to(jnp.arange(n, dtype=jnp.int32)[:, None], (n, TOP_K))
    slot_tok = jnp.zeros((nt * TR,), jnp.int32).at[dest.reshape(-1)].set(tok.reshape(-1))
    ti = jnp.arange(nt, dtype=jnp.int32)
    e_of = jnp.minimum(jnp.sum((t_ends[None, :] <= ti[:, None]).astype(jnp.int32), axis=1), N_EXPERTS - 1)
    tile_ns = jnp.where(ti < n_tiles, jnp.clip(n_sub[e_of] - 2 * (ti - t_starts[e_of]), 0, 2), 0)
    tile_e = jnp.where(ti < n_tiles, e_of, e_of[n_tiles - 1])

    ys = _moe(tile_e.astype(jnp.int32), tile_ns.astype(jnp.int32), slot_tok.reshape(nt, 1, TR), xq,
              w_gu, b_gu, w_down, b_down[:, None, :])

    out = _combine(dest.reshape(n // TM_OUT, 1, TM_OUT * TOP_K), h_mid, ys, rt, p2, norm_ple[None],
                   w_ple_gate.astype(BF16), w_ple_proj.astype(BF16), norm_out[None])
    return out


def kernel(x, p, norm_mix, w_in, conv_a_w, w_a_out, conv_b_w, conv_b_b, w_rg, b_rg, w_ig, b_ig, lru_lambda, w_b_out, w_o, norm_ffn, w_router, b_router, w_gu, b_gu, w_down, b_down, norm_ple, w_ple_gate, w_ple_proj, norm_final):
    bsz, t, d = x.shape
    depth = p.shape[0]
    assert bsz == 1 and depth == 1, "kernel fuses the final norm into the single layer"
    out = _layer(x.reshape(t, d), p[0].reshape(t, -1), norm_mix[0], w_in[0], conv_a_w[0], w_a_out[0],
                 conv_b_w[0], conv_b_b[0], w_rg[0], b_rg[0], w_ig[0], b_ig[0], lru_lambda[0], w_b_out[0],
                 w_o[0], norm_ffn[0], w_router[0], b_router[0], w_gu[0], b_gu[0], w_down[0], b_down[0],
                 norm_ple[0], w_ple_gate[0], w_ple_proj[0], norm_final)
    return out.reshape(bsz, t, d)
```

```python
import functools

import jax
import jax.numpy as jnp
from jax import lax
from jax.experimental import pallas as pl
from jax.experimental.pallas import tpu as pltpu

F32 = jnp.float32
BF16 = jnp.bfloat16

NORM_EPS = 1e-6
LRU_HEADS = 8
LRU_C = 8.0
N_EXPERTS = 32
TOP_K = 4
SWIGLU_LIMIT = 7.0
SWIGLU_ALPHA = 1.702

LANES = 128
SUBLANES = 8
VMEM_LIMIT = 56 * 1024 * 1024

TM_MIX = 512
TM_OUT = 256
TR = 1024
FC = 256
MIX1_CPS = 2
MIX1_ROW_SPLIT = 1
MOE_CHAIN = 128
MIX2_CHAIN = 256
NEG_BIG = -3.0e38


def _rms(x, g):
    ms = jnp.mean(x * x, axis=-1, keepdims=True)
    return x * lax.rsqrt(ms + NORM_EPS) * g


def _dot(a, b):
    return jnp.dot(a, b, preferred_element_type=F32)


def _shift_rows(u, hist8, k):
    rolled = pltpu.roll(u, k, axis=0)
    hr = pltpu.roll(hist8, k, axis=0)
    row = lax.broadcasted_iota(jnp.int32, hist8.shape, 0)
    first = jnp.where(row < k, hr, rolled[:SUBLANES])
    return jnp.concatenate([first, rolled[SUBLANES:]], axis=0)


def _gelu_tanh(x):
    return 0.5 * x * (1.0 + jnp.tanh(0.7978845608028654 * (x + 0.044715 * (x * x * x))))


def _mix1_kernel(x_ref, gmix_ref, wpb_ref, wpc_ref, wpv_ref, wpy_ref, wpx_ref, par_ref, wgate_ref,
                 a_ref, yc_ref, hy_ref,
                 hu_s, hx_s, hp_s, h_s):
    i = pl.program_id(0)
    c = pl.program_id(1)
    tm = yc_ref.shape[0]
    cps = wgate_ref.shape[0]
    cw = yc_ref.shape[1] // cps

    @pl.when(jnp.logical_and(i == 0, c == 0))
    def _():
        hu_s[...] = jnp.zeros_like(hu_s)
        hx_s[...] = jnp.zeros_like(hx_s)
        hp_s[...] = jnp.zeros_like(hp_s)

    @pl.when(c == 0)
    def _():
        a_ref[...] = _rms(x_ref[...], gmix_ref[...]).astype(BF16)

    rt = tm // MIX1_ROW_SPLIT
    trow = lax.broadcasted_iota(jnp.int32, (rt, cw), 0)
    sub = trow % SUBLANES

    for q in range(cps):
        ch = c * cps + q
        cols = slice(q * cw, (q + 1) * cw)
        par = par_ref[:, cols]
        nlam = -par[10:11]
        softplus = jnp.maximum(nlam, 0.0) + jnp.log1p(jnp.exp(-jnp.abs(nlam)))
        hu = hu_s[ch]
        hx = hx_s[ch]
        carry = hp_s[ch][SUBLANES - 1:SUBLANES, :]
        for part in range(MIX1_ROW_SPLIT):
            rows = slice(part * rt, (part + 1) * rt)
            a_bf = a_ref[rows, :]
            x_r = _dot(a_bf, wpx_ref[:, cols])
            b_c = _dot(a_bf, wpb_ref[:, cols])
            c_c = _dot(a_bf, wpc_ref[:, cols])
            v_c = _dot(a_bf, wpv_ref[:, cols])
            y_r = _dot(a_bf, wpy_ref[:, cols])

            xc = (par[3:4] * _shift_rows(x_r, hx, 3) + par[4:5] * _shift_rows(x_r, hx, 2)
                  + par[5:6] * _shift_rows(x_r, hx, 1) + par[6:7] * x_r) + par[7:8]
            hx = x_r[rt - SUBLANES:, :]
            gz = _dot(xc.astype(BF16), wgate_ref[q])

            u = c_c * v_c
            conv = (par[0:1] * _shift_rows(u, hu, 2) + par[1:2] * _shift_rows(u, hu, 1) + par[2:3] * u)
            yc_ref[rows, cols] = (b_c * conv).astype(BF16)
            hu = u[rt - SUBLANES:, :]

            r = jax.nn.sigmoid(gz[:, :cw] + par[8:9])
            ig = jax.nn.sigmoid(gz[:, cw:] + par[9:10])
            log_a = (-LRU_C) * r * softplus
            a = jnp.exp(log_a)
            mult = jnp.sqrt(1.0 - a * a)
            if part == 0:
                mult = jnp.where(trow + i * tm == 0, 1.0, mult)
            b = xc * ig * mult

            for s in (1, 2, 4):
                a_sh = pltpu.roll(a, s, axis=0)
                b_sh = pltpu.roll(b, s, axis=0)
                m = sub >= s
                b = jnp.where(m, a * b_sh + b, b)
                a = jnp.where(m, a * a_sh, a)
            for g in range(rt // SUBLANES):
                lo = g * SUBLANES
                hg = b[lo:lo + SUBLANES, :] + a[lo:lo + SUBLANES, :] * carry
                h_s[q, part * rt + lo:part * rt + lo + SUBLANES, :] = hg
                carry = hg[SUBLANES - 1:SUBLANES, :]
            hy_ref[rows, cols] = (h_s[q, rows, :] * _gelu_tanh(y_r)).astype(BF16)
        hu_s[ch] = hu
        hx_s[ch] = hx
        hp_s[ch] = h_s[q, tm - SUBLANES:, :]


def _mix1(x2, gmix, w_in_bf, par, wgate):
    n, d = x2.shape
    nch, cw, _ = wgate.shape
    tm = TM_MIX
    cps = MIX1_CPS
    steps = nch // cps
    grid = (n // tm, steps)
    split = lambda s: pl.BlockSpec((d, cps * cw), lambda i, c: (0, s * steps + c))
    return pl.pallas_call(
        _mix1_kernel,
        grid=grid,
        in_specs=[
            pl.BlockSpec((tm, d), lambda i, c: (i, 0)),
            pl.BlockSpec((1, d), lambda i, c: (0, 0)),
            split(0), split(1), split(2), split(3), split(4),
            pl.BlockSpec((16, cps * cw), lambda i, c: (0, c)),
            pl.BlockSpec((cps, cw, 2 * cw), lambda i, c: (c, 0, 0)),
        ],
        out_specs=[
            pl.BlockSpec((tm, d), lambda i, c: (i, 0)),
            pl.BlockSpec((tm, cps * cw), lambda i, c: (i, c)),
            pl.BlockSpec((tm, cps * cw), lambda i, c: (i, c)),
        ],
        out_shape=[
            jax.ShapeDtypeStruct((n, d), BF16),
            jax.ShapeDtypeStruct((n, d), BF16),
            jax.ShapeDtypeStruct((n, d), BF16),
        ],
        scratch_shapes=[
            pltpu.VMEM((nch, SUBLANES, cw), F32),
            pltpu.VMEM((nch, SUBLANES, cw), F32),
            pltpu.VMEM((nch, SUBLANES, cw), F32),
            pltpu.VMEM((cps, tm, cw), F32),
        ],
        compiler_params=pltpu.CompilerParams(
            dimension_semantics=("arbitrary", "arbitrary"), vmem_limit_bytes=VMEM_LIMIT),
        name="mix1",
    )(x2, gmix, w_in_bf, w_in_bf, w_in_bf, w_in_bf, w_in_bf, par, wgate)


def _mix2_kernel(x_ref, a_ref, yc_ref, hy_ref, wgc_ref, wgr_ref, wa_ref, wb_ref, wo_ref,
                 gffn_ref, wrh_ref, wrl_ref, br_ref,
                 h_ref, xq_ref, rt_ref, cnt_ref,
                 cnt_s):
    i = pl.program_id(0)
    kc = pl.program_id(1)
    nkc = pl.num_programs(1)
    tm = x_ref.shape[0]
    cw = wa_ref.shape[1]

    @pl.when(jnp.logical_and(i == 0, kc == 0))
    def _():
        cnt_s[...] = jnp.zeros_like(cnt_s)

    @pl.when(kc == 0)
    def _():
        h_ref[...] = x_ref[...]

    wgc = wgc_ref[...]
    wgr = wgr_ref[...]
    wa = wa_ref[...]
    wb = wb_ref[...]
    wo = wo_ref[0]
    chains = [slice(c * MIX2_CHAIN, (c + 1) * MIX2_CHAIN) for c in range(tm // MIX2_CHAIN)]
    pre = []
    for rs in chains:
        g_conv = _dot(a_ref[rs, :], wgc)
        g_rec = _dot(a_ref[rs, :], wgr)
        y_conv = _dot(yc_ref[rs, :], wa)
        y_rec = _dot(hy_ref[rs, :], wb)
        pre.append((g_conv, g_rec, y_conv, y_rec))
    for rs, (g_conv, g_rec, y_conv, y_rec) in zip(chains, pre):
        merged = jax.nn.sigmoid(g_conv) * y_conv + jax.nn.sigmoid(g_rec) * y_rec
        h_ref[rs, :] += _dot(merged.astype(BF16), wo)

    @pl.when(kc == nkc - 1)
    def _():
        xn = _rms(h_ref[...], gffn_ref[...])
        xh = xn.astype(BF16)
        xhf = xh.astype(F32)
        xl = (xn - xhf).astype(BF16)
        half = xn.shape[1] // 2
        lo = pltpu.bitcast(xhf[:, :half], jnp.uint32) >> 16
        hi = pltpu.bitcast(xhf[:, half:], jnp.uint32) & jnp.uint32(0xFFFF0000)
        words = lo | hi
        tile_rows = half // LANES
        for q in range(tile_rows):
            xq_ref[pl.ds(q, tm, stride=tile_rows), :] = words[:, q * LANES:(q + 1) * LANES]
        wrh = wrh_ref[...]
        logits = _dot(xh, wrh) + _dot(xh, wrl_ref[...]) + _dot(xl, wrh) + br_ref[...]
        lane = lax.broadcasted_iota(jnp.int32, (tm, LANES), 1)
        lane_f = lane.astype(F32)
        work = jnp.where(lane < N_EXPERTS, logits, NEG_BIG)
        vals, idxs, hots = [], [], []
        for _k in range(TOP_K):
            m = jnp.max(work, axis=-1, keepdims=True)
            idx = jnp.min(jnp.where(work == m, lane_f, float(LANES)), axis=-1, keepdims=True)
            hot = lane_f == idx
            work = jnp.where(hot, NEG_BIG, work)
            vals.append(m)
            idxs.append(idx)
            hots.append(hot)
        es = [jnp.exp(v - vals[0]) for v in vals]
        denom = es[0] + es[1] + es[2] + es[3]
        ws = [e / denom for e in es]
        hot_any = jnp.zeros((tm, LANES), F32)
        for hot in hots:
            hot_any = hot_any + hot.astype(F32)
        rr = lax.broadcasted_iota(jnp.int32, (tm, tm), 0)
        cc = lax.broadcasted_iota(jnp.int32, (tm, tm), 1)
        lower = jnp.where(rr > cc, 1.0, 0.0).astype(BF16)
        cum = _dot(lower, hot_any.astype(BF16)) + cnt_s[0:1, :]
        out = jnp.zeros((tm, LANES), F32)
        for k in range(TOP_K):
            rank = jnp.sum(jnp.where(hots[k], cum, 0.0), axis=-1, keepdims=True)
            out = jnp.where(lane == k, idxs[k], out)
            out = jnp.where(lane == TOP_K + k, ws[k], out)
            out = jnp.where(lane == 2 * TOP_K + k, rank, out)
        rt_ref[...] = out
        cnt_new = cnt_s[...] + jnp.sum(hot_any, axis=0, keepdims=True)
        cnt_s[...] = cnt_new
        cnt_ref[...] = cnt_new


def _mix2(x2, a, yc, hy, w_in_bf, wa, wb, wo, gffn, wrh, wrl, br):
    n, d = x2.shape
    nkc, cw, _ = wo.shape
    tm = TM_MIX
    grid = (n // tm, nkc)
    row = lambda i, k: (i, 0)
    const = lambda i, k: (0, 0)
    chunk = lambda i, k: (k, 0, 0)
    return pl.pallas_call(
        _mix2_kernel,
        grid=grid,
        in_specs=[
            pl.BlockSpec((tm, d), row),
            pl.BlockSpec((tm, d), row),
            pl.BlockSpec((tm, d), row),
            pl.BlockSpec((tm, d), row),
            pl.BlockSpec((d, cw), lambda i, k: (0, 5 * nkc + k)),
            pl.BlockSpec((d, cw), lambda i, k: (0, 6 * nkc + k)),
            pl.BlockSpec((d, cw), lambda i, k: (0, k)),
            pl.BlockSpec((d, cw), lambda i, k: (0, k)),
            pl.BlockSpec((1, cw, d), chunk),
            pl.BlockSpec((1, d), const),
            pl.BlockSpec((d, LANES), const),
            pl.BlockSpec((d, LANES), const),
            pl.BlockSpec((1, LANES), const),
        ],
        out_specs=[
            pl.BlockSpec((tm, d), row),
            pl.BlockSpec((tm * (d // 2 // LANES), LANES), row),
            pl.BlockSpec((tm, LANES), row),
            pl.BlockSpec((SUBLANES, LANES), const),
        ],
        out_shape=[
            jax.ShapeDtypeStruct((n, d), F32),
            jax.ShapeDtypeStruct((n * (d // 2 // LANES), LANES), jnp.uint32),
            jax.ShapeDtypeStruct((n, LANES), F32),
            jax.ShapeDtypeStruct((SUBLANES, LANES), F32),
        ],
        scratch_shapes=[pltpu.VMEM((SUBLANES, LANES), F32)],
        compiler_params=pltpu.CompilerParams(
            dimension_semantics=("arbitrary", "arbitrary"), vmem_limit_bytes=VMEM_LIMIT),
        name="mix2",
    )(x2, a, yc, hy, w_in_bf, w_in_bf, wa, wb, wo, gffn, wrh, wrl, br)


def _moe_kernel(te_ref, ns_ref,
                tok_ref, xq_hbm, wgu_hbm, wd_hbm, bgu_ref, bd_ref,
                o_ref, gbuf, xb, wg_s, wu_s, wd_s, gsem, wsem):
    i = pl.program_id(0)
    nt = pl.num_programs(0) - 1
    tr, d = o_ref.shape
    sb = tr // 2
    fc = wg_s.shape[2]
    f = wd_hbm.shape[1]
    nf = f // fc
    half = d // 2
    tile_rows = half // LANES
    per_chunk = tr // (nf // 2)

    i_n = jnp.minimum(i, nt - 1)
    ns_next = jnp.where(i < nt, ns_ref[i_n], 0)
    slot_n = i % 2
    issue_iters = ns_next * (sb // per_chunk)
    ic = jnp.maximum(i - 1, 0)
    ns_cur = jnp.where(i > 0, ns_ref[ic], 0)
    slot_c = ic % 2
    e_cur = te_ref[ic]

    def start_row(r):
        src = pl.multiple_of(tok_ref[0, 0, r] * tile_rows, tile_rows)
        dst = pl.multiple_of(r * tile_rows, tile_rows)
        pltpu.make_async_copy(xq_hbm.at[pl.ds(src, tile_rows)], gbuf.at[slot_n, pl.ds(dst, tile_rows)],
                              gsem.at[slot_n]).start()

    def weight_copies(e, j, slot):
        col = pl.multiple_of(j * fc, fc)
        ucol = pl.multiple_of(f + j * fc, fc)
        return (
            pltpu.make_async_copy(wgu_hbm.at[e, :, pl.ds(col, fc)], wg_s.at[slot], wsem.at[slot, 0]),
            pltpu.make_async_copy(wgu_hbm.at[e, :, pl.ds(ucol, fc)], wu_s.at[slot], wsem.at[slot, 1]),
            pltpu.make_async_copy(wd_hbm.at[e, pl.ds(col, fc), :], wd_s.at[slot], wsem.at[slot, 2]),
        )

    @pl.when(jnp.logical_and(i == 0, ns_next > 0))
    def _():
        for cp in weight_copies(te_ref[0], 0, 0):
            cp.start(priority=1)

    @pl.when(ns_cur > 0)
    def _():
        rows = ns_cur * (sb * tile_rows)
        pltpu.make_async_copy(xq_hbm.at[pl.ds(0, rows)], gbuf.at[slot_c, pl.ds(0, rows)],
                              gsem.at[slot_c]).wait()

    for sub in range(2):
        lo_r, hi_r = sub * sb, (sub + 1) * sb

        @pl.when(ns_cur > sub)
        def _(lo_r=lo_r, hi_r=hi_r):
            for q in range(tile_rows):
                w = gbuf[slot_c, pl.ds(lo_r * tile_rows + q, sb, stride=tile_rows), :]
                cl = slice(q * LANES, (q + 1) * LANES)
                ch = slice(half + q * LANES, half + (q + 1) * LANES)
                xb[lo_r:hi_r, cl] = pltpu.bitcast(w << 16, F32).astype(BF16)
                xb[lo_r:hi_r, ch] = pltpu.bitcast(w & jnp.uint32(0xFFFF0000), F32).astype(BF16)
            o_ref[lo_r:hi_r, :] = jnp.broadcast_to(bd_ref[0], (sb, d))

        @pl.when(ns_cur <= sub)
        def _(lo_r=lo_r, hi_r=hi_r):
            o_ref[lo_r:hi_r, :] = jnp.zeros((sb, d), o_ref.dtype)

    def chunk_body(j, nrows, with_issue):
        slot = j % 2
        for cp in weight_copies(e_cur, j, slot):
            cp.wait()
        last = j == nf - 1
        j_n = jnp.where(last, 0, j + 1)
        e_n = jnp.where(last, te_ref[i_n], e_cur)

        @pl.when(jnp.logical_or(jnp.logical_not(last), ns_next > 0))
        def _():
            for cp in weight_copies(e_n, j_n, 1 - slot):
                cp.start(priority=1)

        if with_issue:
            for r in range(per_chunk):
                start_row(j * per_chunk + r)
        wgu = jnp.concatenate([wg_s[slot], wu_s[slot]], axis=1).astype(BF16)
        wdn = wd_s[slot].astype(BF16)
        bg = bgu_ref[0, pl.ds(j, 1), :]
        bu = bgu_ref[0, pl.ds(nf + j, 1), :]
        chain = min(MOE_CHAIN, nrows)
        chains = [slice(c * chain, (c + 1) * chain) for c in range(nrows // chain)]
        gus = [_dot(xb[rs, :], wgu) for rs in chains]
        for rs, gu in zip(chains, gus):
            gate = jnp.minimum(gu[:, :fc] + bg, SWIGLU_LIMIT)
            up = jnp.clip(gu[:, fc:] + bu, -SWIGLU_LIMIT, SWIGLU_LIMIT)
            glu = gate * jax.nn.sigmoid(SWIGLU_ALPHA * gate)
            act = ((up + 1.0) * glu).astype(BF16)
            o_ref[rs, :] += _dot(act, wdn)

    def run_tile(nrows):
        def with_issue(j, carry):
            chunk_body(j, nrows, True)
            return carry

        def without_issue(j, carry):
            chunk_body(j, nrows, False)
            return carry

        lax.fori_loop(0, issue_iters, with_issue, 0)
        lax.fori_loop(issue_iters, nf, without_issue, 0)

    for nsub in (1, 2):
        @pl.when(ns_cur == nsub)
        def _(nsub=nsub):
            run_tile(nsub * sb)

    @pl.when(jnp.logical_and(ns_cur == 0, ns_next > 0))
    def _():
        def issue(g, carry):
            for r in range(SUBLANES):
                start_row(g * SUBLANES + r)
            return carry

        lax.fori_loop(0, ns_next * (sb // SUBLANES), issue, 0)


def _moe(tile_e, tile_ns, slot_tok3, xq, wgu, bgu, wd, bd3):
    d = wgu.shape[1]
    f = wd.shape[1]
    half = d // 2
    nt, _, tr = slot_tok3.shape
    nf = f // FC
    bgu3 = bgu.reshape(bgu.shape[0], 2 * nf, FC)

    def cur(i):
        return jnp.maximum(i - 1, 0)

    grid_spec = pltpu.PrefetchScalarGridSpec(
        num_scalar_prefetch=2,
        grid=(nt + 1,),
        in_specs=[
            pl.BlockSpec((1, 1, tr), lambda i, te, ns: (jnp.minimum(i, nt - 1), 0, 0),
                         memory_space=pltpu.SMEM),
            pl.BlockSpec(memory_space=pl.ANY),
            pl.BlockSpec(memory_space=pl.ANY),
            pl.BlockSpec(memory_space=pl.ANY),
            pl.BlockSpec((1, 2 * nf, FC), lambda i, te, ns: (te[cur(i)], 0, 0)),
            pl.BlockSpec((1, 1, d), lambda i, te, ns: (te[cur(i)], 0, 0)),
        ],
        out_specs=pl.BlockSpec((tr, d), lambda i, te, ns: (cur(i), 0)),
        scratch_shapes=[
            pltpu.VMEM((2, tr * (half // LANES), LANES), jnp.uint32),
            pltpu.VMEM((tr, d), BF16),
            pltpu.VMEM((2, d, FC), F32),
            pltpu.VMEM((2, d, FC), F32),
            pltpu.VMEM((2, FC, d), F32),
            pltpu.SemaphoreType.DMA((2,)),
            pltpu.SemaphoreType.DMA((2, 3)),
        ],
    )
    return pl.pallas_call(
        _moe_kernel,
        grid_spec=grid_spec,
        out_shape=jax.ShapeDtypeStruct((nt * tr, d), F32),
        compiler_params=pltpu.CompilerParams(
            dimension_semantics=("arbitrary",), vmem_limit_bytes=VMEM_LIMIT),
        name="moe",
    )(tile_e, tile_ns, slot_tok3, xq, wgu, wd, bgu3, bd3)


def _combine_kernel(dst_ref, h_ref, ys_hbm, rt_ref, p_ref, gple_ref, wpg_ref, wpp_ref, gfin_ref,
                    o_ref, ybuf_a, ybuf_b, sem):
    i = pl.program_id(0)
    nt = pl.num_programs(0) - 1
    tm = h_ref.shape[0]
    bufs = (ybuf_a, ybuf_b)

    def start_token(r, slot):
        for k in range(TOP_K):
            s = dst_ref[0, 0, r * TOP_K + k]
            pltpu.make_async_copy(ys_hbm.at[pl.ds(s, 1)], bufs[slot].at[k, pl.ds(r, 1)],
                                  sem.at[slot]).start(priority=k % 2)

    @pl.when(i == 0)
    def _():
        def issue(r, carry):
            start_token(r, 0)
            return carry

        lax.fori_loop(0, tm, issue, 0, unroll=4)

    def compute(slot, with_issue):
        ybuf = bufs[slot]
        for k in range(TOP_K):
            pltpu.make_async_copy(ys_hbm.at[pl.ds(0, tm)], ybuf.at[k], sem.at[slot]).wait()
        if with_issue:
            for r in range(tm):
                start_token(r, 1 - slot)
        rt = rt_ref[...]
        y = rt[:, TOP_K:TOP_K + 1] * ybuf[0]
        for k in range(1, TOP_K):
            y = y + rt[:, TOP_K + k:TOP_K + k + 1] * ybuf[k]
        h2 = h_ref[...] + y
        gate = jax.nn.sigmoid(_dot(_rms(h2, gple_ref[...]).astype(BF16), wpg_ref[...]))
        pp = _dot(p_ref[...].astype(BF16), wpp_ref[...])
        h3 = h2 + gate * pp
        o_ref[...] = _rms(h3, gfin_ref[...])

    for slot in range(2):
        @pl.when(jnp.logical_and(jnp.logical_and(i > 0, i < nt), (i - 1) % 2 == slot))
        def _(slot=slot):
            compute(slot, True)

        @pl.when(jnp.logical_and(i == nt, (i - 1) % 2 == slot))
        def _(slot=slot):
            compute(slot, False)


def _combine(dest3, h, ys, rt, p2, gple, wpg, wpp, gfin):
    n, d = h.shape
    tm = TM_OUT
    nt = n // tm
    pd = p2.shape[1]
    cur = lambda i: (jnp.maximum(i - 1, 0), 0)
    const = lambda i: (0, 0)
    return pl.pallas_call(
        _combine_kernel,
        grid=(nt + 1,),
        in_specs=[
            pl.BlockSpec((1, 1, tm * TOP_K), lambda i: (jnp.minimum(i, nt - 1), 0, 0),
                         memory_space=pltpu.SMEM),
            pl.BlockSpec((tm, d), cur),
            pl.BlockSpec(memory_space=pl.ANY),
            pl.BlockSpec((tm, LANES), cur),
            pl.BlockSpec((tm, pd), cur),
            pl.BlockSpec((1, d), const),
            pl.BlockSpec((d, d), const),
            pl.BlockSpec((pd, d), const),
            pl.BlockSpec((1, d), const),
        ],
        out_specs=pl.BlockSpec((tm, d), cur),
        out_shape=jax.ShapeDtypeStruct((n, d), F32),
        scratch_shapes=[
            pltpu.VMEM((TOP_K, tm, d), F32),
            pltpu.VMEM((TOP_K, tm, d), F32),
            pltpu.SemaphoreType.DMA((2,)),
        ],
        compiler_params=pltpu.CompilerParams(
            dimension_semantics=("arbitrary",), vmem_limit_bytes=VMEM_LIMIT),
        name="combine",
    )(dest3, h, ys, rt, p2, gple, wpg, wpp, gfin)


def _layer(h2, p2, norm_mix, w_in, conv_a_w, w_a_out, conv_b_w, conv_b_b, w_rg, b_rg, w_ig, b_ig,
           lru_lambda, w_b_out, w_o, norm_ffn, w_router, b_router, w_gu, b_gu, w_down, b_down,
           norm_ple, w_ple_gate, w_ple_proj, norm_out):
    n, d = h2.shape
    nch = LRU_HEADS
    cw = d // nch

    w_in_bf = w_in.astype(BF16)
    wa = w_a_out.astype(BF16)
    wb = w_b_out.astype(BF16)
    wo = w_o.reshape(nch, cw, d).astype(BF16)
    wgate = jnp.concatenate([w_rg, w_ig], axis=-1).astype(BF16)
    par = jnp.concatenate([conv_a_w, conv_b_w, conv_b_b[None], b_rg[None], b_ig[None],
                           lru_lambda[None], jnp.zeros((16 - 11, d), F32)], axis=0)
    wr_pad = jnp.pad(w_router, ((0, 0), (0, LANES - N_EXPERTS)))
    wrh = wr_pad.astype(BF16)
    wrl = (wr_pad - wrh.astype(F32)).astype(BF16)
    br = jnp.pad(b_router, (0, LANES - N_EXPERTS))[None]

    a, yc, hy = _mix1(h2, norm_mix[None], w_in_bf, par, wgate)
    h_mid, xq, rt, cnt = _mix2(h2, a, yc, hy, w_in_bf, wa, wb, wo, norm_ffn[None], wrh, wrl, br)

    top_idx = rt[:, 0:TOP_K].astype(jnp.int32)
    rank = rt[:, 2 * TOP_K:3 * TOP_K].astype(jnp.int32)
    counts = cnt[0, :N_EXPERTS].astype(jnp.int32)
    sb = TR // 2
    nt = -(-(n * TOP_K) // TR) + N_EXPERTS
    n_sub = (counts + sb - 1) // sb
    n_tile = (n_sub + 1) // 2
    t_ends = jnp.cumsum(n_tile)
    t_starts = t_ends - n_tile
    n_tiles = t_ends[-1]
    dest = (t_starts * TR)[top_idx] + rank
    tok = jnp.broadcast_to(jnp.arange(n, dtype=jnp.int32)[:, None], (n, TOP_K))
    slot_tok = jnp.zeros((nt * TR,), jnp.int32).at[dest.reshape(-1)].set(tok.reshape(-1))
    ti = jnp.arange(nt, dtype=jnp.int32)
    e_of = jnp.minimum(jnp.sum((t_ends[None, :] <= ti[:, None]).astype(jnp.int32), axis=1), N_EXPERTS - 1)
    tile_ns = jnp.where(ti < n_tiles, jnp.clip(n_sub[e_of] - 2 * (ti - t_starts[e_of]), 0, 2), 0)
    tile_e = jnp.where(ti < n_tiles, e_of, e_of[n_tiles - 1])

    ys = _moe(tile_e.astype(jnp.int32), tile_ns.astype(jnp.int32), slot_tok.reshape(nt, 1, TR), xq,
              w_gu, b_gu, w_down, b_down[:, None, :])

    out = _combine(dest.reshape(n // TM_OUT, 1, TM_OUT * TOP_K), h_mid, ys, rt, p2, norm_ple[None],
                   w_ple_gate.astype(BF16), w_ple_proj.astype(BF16), norm_out[None])
    return out


def kernel(x, p, norm_mix, w_in, conv_a_w, w_a_out, conv_b_w, conv_b_b, w_rg, b_rg, w_ig, b_ig, lru_lambda, w_b_out, w_o, norm_ffn, w_router, b_router, w_gu, b_gu, w_down, b_down, norm_ple, w_ple_gate, w_ple_proj, norm_final):
    bsz, t, d = x.shape
    depth = p.shape[0]
    assert bsz == 1 and depth == 1, "kernel fuses the final norm into the single layer"
    out = _layer(x.reshape(t, d), p[0].reshape(t, -1), norm_mix[0], w_in[0], conv_a_w[0], w_a_out[0],
                 conv_b_w[0], conv_b_b[0], w_rg[0], b_rg[0], w_ig[0], b_ig[0], lru_lambda[0], w_b_out[0],
                 w_o[0], norm_ffn[0], w_router[0], b_router[0], w_gu[0], b_gu[0], w_down[0], b_down[0],
                 norm_ple[0], w_ple_gate[0], w_ple_proj[0], norm_final)
    return out.reshape(bsz, t, d)
```
